```python
import math
import jax
import jax.numpy as jnp
from jax import lax
import numpy as np

D_MODEL = 1024
BATCH = 4
SEQ = 4096
DEPTH = 1
DEC_BATCH = 32
DEC_SEQ = 4
PAST_LEN = 8192
PAGE_SIZE = 128

DA_HEADS = 8
DA_HEAD_DIM = 64
DA_V_DIM = 2 * DA_HEAD_DIM
DA_WIDTH = DA_HEADS * DA_V_DIM
RET_HEADS = 4
RET_K_DIM = 128
RET_V_DIM = 256
RET_QK_WIDTH = RET_HEADS * RET_K_DIM
RET_WIDTH = RET_HEADS * RET_V_DIM
D_FF = 2816
Q_BLOCK = 128
RET_CHUNK = 128
LN_EPS = 1e-5
ALPHA = (2.0 * DEPTH) ** 0.25
BETA = (8.0 * DEPTH) ** -0.25
IN_SPLITS = (DA_WIDTH, DA_WIDTH, DA_WIDTH, RET_QK_WIDTH, RET_QK_WIDTH, RET_WIDTH, RET_WIDTH, D_MODEL, D_MODEL)
IN_WIDTH = sum(IN_SPLITS)
IN_OFFSETS = [int(o) for o in np.cumsum(IN_SPLITS)[:-1]]

kernel_name = "hybrid_diffattn_retention_macaron_step"


def lambda_init(layer):
    return 0.8 - 0.6 * math.exp(-0.3 * layer)


def alibi_slopes():
    return jnp.asarray((2.0 ** (-8.0 * np.arange(1, DA_HEADS + 1) / DA_HEADS)).astype(np.float32))


def retention_log_decay():
    return jnp.asarray(np.log(1.0 - 2.0 ** (-5.0 - np.arange(RET_HEADS))).astype(np.float32))


def layer_norm(x, g, b):
    x32 = x.astype(jnp.float32)
    mu = x32.mean(-1, keepdims=True)
    var = jnp.square(x32 - mu).mean(-1, keepdims=True)
    return ((x32 - mu) * lax.rsqrt(var + LN_EPS) * g + b).astype(x.dtype)


def rms_norm(x):
    x32 = x.astype(jnp.float32)
    return x32 * lax.rsqrt(jnp.mean(jnp.square(x32), axis=-1, keepdims=True) + LN_EPS)


def swiglu(x, w_gu, w_down):
    a, u = jnp.split(x @ w_gu, 2, axis=-1)
    return (jax.nn.silu(a) * u) @ w_down


def ffn_sublayer(x, w_gu, w_down, g, b):
    return layer_norm(ALPHA * x + 0.5 * swiglu(x, w_gu, w_down), g, b)


def split_inputs(x, w_in):
    B, T = x.shape[:2]
    qa, ka, va, qr, kr, vr, gr, ga, gb = jnp.split(x @ w_in, IN_OFFSETS, axis=-1)
    qa = qa.reshape(B, T, DA_HEADS, 2, DA_HEAD_DIM)
    ka = ka.reshape(B, T, DA_HEADS, 2 * DA_HEAD_DIM)
    va = va.reshape(B, T, DA_HEADS, DA_V_DIM)
    qr = qr.reshape(B, T, RET_HEADS, RET_K_DIM)
    kr = kr.reshape(B, T, RET_HEADS, RET_K_DIM) * (RET_K_DIM ** -0.5)
    vr = vr.reshape(B, T, RET_HEADS, RET_V_DIM)
    return qa, ka, va, qr, kr, vr, gr, ga, gb


def diff_lambda(lp, lam_init):
    lp = lp.astype(jnp.float32)
    return jnp.exp(jnp.sum(lp[0] * lp[1])) - jnp.exp(jnp.sum(lp[2] * lp[3])) + lam_init


def diff_attention(q, k, v, q_pos, k_pos, lam):
    B, Tk = k.shape[:2]
    k = k.reshape(B, Tk, DA_HEADS, 2, DA_HEAD_DIM)
    s = jnp.einsum('bqhmd,bkhmd->bhmqk', q, k, preferred_element_type=jnp.float32) * (DA_HEAD_DIM ** -0.5)
    dist = jnp.abs(q_pos[:, None] - k_pos[None, :]).astype(jnp.float32)
    bias = -alibi_slopes()[:, None, None] * dist
    causal = k_pos[None, :] <= q_pos[:, None]
    s = jnp.where(causal, s + bias[:, None], -jnp.inf)
    p = jax.nn.softmax(s, axis=-1)
    a = p[:, :, 0] - lam * p[:, :, 1]
    return jnp.einsum('bhqk,bkhd->bqhd', a, v)


def diff_attention_prompt(q, k, v, lam):
    B, S = q.shape[:2]
    nqb = S // Q_BLOCK
    qb = q.reshape(B, nqb, Q_BLOCK, DA_HEADS, 2, DA_HEAD_DIM).swapaxes(0, 1)
    pos = jnp.arange(S)
    qpos = pos.reshape(nqb, Q_BLOCK)
    out = lax.map(lambda a: diff_attention(a[0], k, v, a[1], pos, lam), (qb, qpos))
    return out.swapaxes(0, 1).reshape(B, S, DA_HEADS, DA_V_DIM)


def diff_post(o, subln_g, lam_init):
    B, T = o.shape[:2]
    return (rms_norm(o) * subln_g * (1.0 - lam_init)).reshape(B, T, DA_WIDTH)


def retention_chunk(state, q, k, v):
    log_g = retention_log_decay()
    q, k, v, state = (t.astype(jnp.float32) for t in (q, k, v, state))
    C = q.shape[1]
    idx = jnp.arange(C, dtype=jnp.float32)
    diff = idx[:, None] - idx[None, :]
    decay = jnp.where(diff >= 0, jnp.exp(jnp.maximum(diff, 0.0)[None] * log_g[:, None, None]), 0.0)
    scores = jnp.einsum('bihd,bjhd->bhij', q, k) * decay
    cross_decay = jnp.exp((idx + 1.0)[:, None] * log_g[None, :])[None, :, :, None]
    o = jnp.einsum('bhij,bjhe->bihe', scores, v) + jnp.einsum('bihd,bhde->bihe', q, state) * cross_decay
    k_w = k * jnp.exp((C - 1.0 - idx)[:, None] * log_g[None, :])[None, :, :, None]
    new_state = jnp.exp(C * log_g)[None, :, None, None] * state + jnp.einsum('bjhd,bjhe->bhde', k_w, v)
    return new_state, o


def retention_prompt(q, k, v):
    B, S = q.shape[:2]
    nc = S // RET_CHUNK
    to_chunks = lambda t: t.astype(jnp.float32).reshape(B, nc, RET_CHUNK, *t.shape[2:]).swapaxes(0, 1)
    s0 = jnp.zeros((B, RET_HEADS, RET_K_DIM, RET_V_DIM), jnp.float32)
    s_fin, o = lax.scan(lambda st, c: retention_chunk(st, *c), s0, (to_chunks(q), to_chunks(k), to_chunks(v)))
    return o.swapaxes(0, 1).reshape(B, S, RET_HEADS, RET_V_DIM), s_fin


def ret_post(o, gr):
    B, T = o.shape[:2]
    return rms_norm(o).reshape(B, T, RET_WIDTH) * jax.nn.silu(gr.astype(jnp.float32))


def merge_sublayer(x, oa, ob, ga, gb, w_pa, w_pb, w_o, g, b):
    m = jax.nn.sigmoid(ga) * (oa @ w_pa) + jax.nn.sigmoid(gb) * (ob @ w_pb)
    return layer_norm(ALPHA * x + m @ w_o, g, b)


def setup_inputs(seed: int = 0) -> dict:
    key = jax.random.key(seed)
    ks = jax.random.split(key, 20)
    n_pages = PAST_LEN // PAGE_SIZE
    n_phys = (DEC_BATCH * n_pages * 5) // 4
    nrm = lambda k, shape, scale: jax.random.normal(k, shape, jnp.float32) * scale
    col_scale = np.ones((IN_WIDTH,), np.float32)
    col_scale[IN_OFFSETS[1]:IN_OFFSETS[2]] = BETA
    col_scale[IN_OFFSETS[4]:IN_OFFSETS[5]] = BETA
    page_table = jax.random.permutation(ks[5], n_phys)[: DEC_BATCH * n_pages].reshape(DEC_BATCH, n_pages).astype(jnp.int32)
    return {
        "x_prompt": nrm(ks[0], (BATCH, SEQ, D_MODEL), 1.0),
        "x_sample": nrm(ks[1], (DEC_BATCH, DEC_SEQ, D_MODEL), 1.0),
        "cache_k": nrm(ks[2], (DEPTH, n_phys, PAGE_SIZE, DA_HEADS, 2 * DA_HEAD_DIM), 1.0),
        "cache_v": nrm(ks[3], (DEPTH, n_phys, PAGE_SIZE, DA_HEADS, DA_V_DIM), BETA),
        "state_ret": nrm(ks[4], (DEPTH, DEC_BATCH, RET_HEADS, RET_K_DIM, RET_V_DIM), 1.0),
        "page_table": page_table,
        "ln_g": 1.0 + nrm(ks[6], (DEPTH, 3, D_MODEL), 0.02),
        "ln_b": nrm(ks[7], (DEPTH, 3, D_MODEL), 0.02),
        "ffn1_w_gu": nrm(ks[8], (DEPTH, D_MODEL, 2 * D_FF), BETA * D_MODEL ** -0.5),
        "ffn1_w_down": nrm(ks[9], (DEPTH, D_FF, D_MODEL), BETA * D_FF ** -0.5),
        "w_in": nrm(ks[10], (DEPTH, D_MODEL, IN_WIDTH), D_MODEL ** -0.5) * jnp.asarray(col_scale),
        "da_lambda": nrm(ks[11], (DEPTH, 4, DA_HEAD_DIM), 0.1),
        "da_subln_g": 1.0 + nrm(ks[12], (DEPTH, DA_V_DIM), 0.02),
        "w_branch_a": nrm(ks[13], (DEPTH, DA_WIDTH, D_MODEL), DA_WIDTH ** -0.5),
        "w_branch_b": nrm(ks[14], (DEPTH, RET_WIDTH, D_MODEL), RET_WIDTH ** -0.5),
        "w_out": nrm(ks[15], (DEPTH, D_MODEL, D_MODEL), BETA * D_MODEL ** -0.5),
        "ffn2_w_gu": nrm(ks[16], (DEPTH, D_MODEL, 2 * D_FF), BETA * D_MODEL ** -0.5),
        "ffn2_w_down": nrm(ks[17], (DEPTH, D_FF, D_MODEL), BETA * D_FF ** -0.5),
    }


def reference(x_prompt, x_sample, cache_k, cache_v, state_ret, page_table, ln_g, ln_b, ffn1_w_gu, ffn1_w_down,
              w_in, da_lambda, da_subln_g, w_branch_a, w_branch_b, w_out, ffn2_w_gu, ffn2_w_down):
    xp, xs = x_prompt, x_sample
    n_dec, t_dec = xs.shape[:2]
    q_pos_s = PAST_LEN + jnp.arange(t_dec)
    k_pos_s = jnp.arange(PAST_LEN + t_dec)
    kp_l, vp_l, sp_l, ks_l, vs_l, ss_l = [], [], [], [], [], []
    for l in range(DEPTH):
        lam_init = lambda_init(l)
        lam = diff_lambda(da_lambda[l], lam_init)
        xp = ffn_sublayer(xp, ffn1_w_gu[l], ffn1_w_down[l], ln_g[l, 0], ln_b[l, 0])
        qa, ka, va, qr, kr, vr, gr, ga, gb = split_inputs(xp, w_in[l])
        oa = diff_post(diff_attention_prompt(qa, ka, va, lam), da_subln_g[l], lam_init)
        o_r, s_fin = retention_prompt(qr, kr, vr)
        ob = ret_post(o_r, gr)
        xp = merge_sublayer(xp, oa, ob, ga, gb, w_branch_a[l], w_branch_b[l], w_out[l], ln_g[l, 1], ln_b[l, 1])
        xp = ffn_sublayer(xp, ffn2_w_gu[l], ffn2_w_down[l], ln_g[l, 2], ln_b[l, 2])
        kp_l.append(ka)
        vp_l.append(va)
        sp_l.append(s_fin)
        xs = ffn_sublayer(xs, ffn1_w_gu[l], ffn1_w_down[l], ln_g[l, 0], ln_b[l, 0])
        qa, ka, va, qr, kr, vr, gr, ga, gb = split_inputs(xs, w_in[l])
        k_past = cache_k[l][page_table].reshape(n_dec, PAST_LEN, DA_HEADS, 2 * DA_HEAD_DIM)
        v_past = cache_v[l][page_table].reshape(n_dec, PAST_LEN, DA_HEADS, DA_V_DIM)
        k_all = jnp.concatenate([k_past, ka.astype(k_past.dtype)], axis=1)
        v_all = jnp.concatenate([v_past, va.astype(v_past.dtype)], axis=1)
        oa = diff_post(diff_attention(qa, k_all, v_all, q_pos_s, k_pos_s, lam), da_subln_g[l], lam_init)
        s_new, o_r = retention_chunk(state_ret[l], qr, kr, vr)
        ob = ret_post(o_r, gr)
        xs = merge_sublayer(xs, oa, ob, ga, gb, w_branch_a[l], w_branch_b[l], w_out[l], ln_g[l, 1], ln_b[l, 1])
        xs = ffn_sublayer(xs, ffn2_w_gu[l], ffn2_w_down[l], ln_g[l, 2], ln_b[l, 2])
        ks_l.append(ka)
        vs_l.append(va)
        ss_l.append(s_new)
    k_prompt = jnp.stack(kp_l)
    v_prompt = jnp.stack(vp_l)
    ret_state_prompt = jnp.stack(sp_l)
    k_sample = jnp.stack(ks_l)
    v_sample = jnp.stack(vs_l)
    ret_state_sample = jnp.stack(ss_l)
    return (xp, xs, k_prompt, v_prompt, ret_state_prompt, k_sample, v_sample, ret_state_sample)
```

```python
import functools
import math

import numpy as np
import jax
import jax.numpy as jnp
from jax import lax
from jax.experimental import pallas as pl
from jax.experimental.pallas import tpu as pltpu

D_MODEL = 1024
PAST_LEN = 8192
PAGE_SIZE = 128
DA_HEADS = 8
DA_HEAD_DIM = 64
DA_V_DIM = 2 * DA_HEAD_DIM
RET_HEADS = 4
RET_K_DIM = 128
RET_V_DIM = 256
D_FF = 2816
LN_EPS = 1e-5
N_GROUPS = 8
G_QA, G_KA, G_VA, G_QKR, G_VR, G_GR, G_GA, G_GB = range(N_GROUPS)

VMEM_LIMIT_BYTES = 48 * 1024 * 1024
NEG_BIG = -1e30

F32 = jnp.float32
BF16 = jnp.bfloat16
_NT = (((1,), (1,)), ((), ()))


def _alibi_slopes():
    return (2.0 ** (-8.0 * np.arange(1, DA_HEADS + 1) / DA_HEADS)).astype(np.float32)


def _retention_log_decay():
    return np.log(1.0 - 2.0 ** (-5.0 - np.arange(RET_HEADS))).astype(np.float32)


def _lambda_init(layer):
    return 0.8 - 0.6 * math.exp(-0.3 * layer)


def _params(*semantics):
    return pltpu.CompilerParams(dimension_semantics=semantics, vmem_limit_bytes=VMEM_LIMIT_BYTES)


def _layer_norm(y, g, b):
    mu = jnp.mean(y, axis=-1, keepdims=True)
    d = y - mu
    var = jnp.mean(d * d, axis=-1, keepdims=True)
    return d * lax.rsqrt(var + LN_EPS) * g + b


def _rms_norm(o):
    return o * lax.rsqrt(jnp.mean(o * o, axis=-1, keepdims=True) + LN_EPS)


def _silu(a):
    return a * jax.nn.sigmoid(a)


def _diff_lambda(lp_ref, lam_init):
    lp = lp_ref[...]
    a = jnp.sum(lp[0:1] * lp[1:2], axis=-1, keepdims=True)
    b = jnp.sum(lp[2:3] * lp[3:4], axis=-1, keepdims=True)
    return jnp.exp(a) - jnp.exp(b) + lam_init


def _ffn_kernel(x_ref, wg_ref, wu_ref, wd_ref, g_ref, b_ref, o_ref, xb_ref, acc_ref, *, alpha):
    f = pl.program_id(1)

    @pl.when(f == 0)
    def _():
        xb_ref[...] = x_ref[...].astype(BF16)
        acc_ref[...] = jnp.zeros_like(acc_ref)

    xb = xb_ref[...]
    gate = jnp.dot(xb, wg_ref[...], preferred_element_type=F32)
    up = jnp.dot(xb, wu_ref[...], preferred_element_type=F32)
    h = (_silu(gate) * up).astype(BF16)
    acc_ref[...] += jnp.dot(h, wd_ref[...], preferred_element_type=F32)

    @pl.when(f == pl.num_programs(1) - 1)
    def _():
        y = alpha * x_ref[...] + 0.5 * acc_ref[...]
        o_ref[...] = _layer_norm(y, g_ref[...], b_ref[...])


def _ffn_sublayer(x, w_gu, w_down, g, b, *, alpha, tm, tf):
    n = x.shape[0]
    nf = D_FF // tf
    return pl.pallas_call(
        functools.partial(_ffn_kernel, alpha=alpha),
        grid=(n // tm, nf),
        in_specs=[
            pl.BlockSpec((tm, D_MODEL), lambda i, f: (i, 0)),
            pl.BlockSpec((D_MODEL, tf), lambda i, f: (0, f)),
            pl.BlockSpec((D_MODEL, tf), lambda i, f: (0, f + nf)),
            pl.BlockSpec((tf, D_MODEL), lambda i, f: (f, 0)),
            pl.BlockSpec((1, D_MODEL), lambda i, f: (0, 0)),
            pl.BlockSpec((1, D_MODEL), lambda i, f: (0, 0)),
        ],
        out_specs=pl.BlockSpec((tm, D_MODEL), lambda i, f: (i, 0)),
        out_shape=jax.ShapeDtypeStruct((n, D_MODEL), F32),
        scratch_shapes=[pltpu.VMEM((tm, D_MODEL), BF16), pltpu.VMEM((tm, D_MODEL), F32)],
        compiler_params=_params("parallel", "arbitrary"),
        name="ffn_sublayer",
    )(x, w_gu, w_gu, w_down, g, b)


def _proj_kernel(x_ref, *refs):
    w_refs = refs[:N_GROUPS]
    scale_ref = refs[N_GROUPS]
    out_refs = refs[N_GROUPS + 1:2 * N_GROUPS + 1]
    k32_ref, v32_ref, xb_ref = refs[2 * N_GROUPS + 1:]

    @pl.when(pl.program_id(1) == 0)
    def _():
        xb_ref[...] = x_ref[...].astype(BF16)

    xb = xb_ref[...]
    for grp in range(N_GROUPS):
        y = jnp.dot(xb, w_refs[grp][...], preferred_element_type=F32)
        if grp == G_QKR:
            y = y * scale_ref[...]
        out_refs[grp][...] = y.astype(out_refs[grp].dtype)
        if grp == G_KA:
            k32_ref[...] = y
        if grp == G_VA:
            v32_ref[...] = y


def _input_projection(x, w_in, qkr_scale, *, act_dtype, tm, tn):
    n = x.shape[0]
    nj = D_MODEL // tn
    w_specs = [pl.BlockSpec((D_MODEL, tn), functools.partial(lambda i, j, grp: (0, grp * nj + j), grp=grp))
               for grp in range(N_GROUPS)]
    out_spec = pl.BlockSpec((tm, tn), lambda i, j: (i, j))
    return pl.pallas_call(
        _proj_kernel,
        grid=(n // tm, nj),
        in_specs=[pl.BlockSpec((tm, D_MODEL), lambda i, j: (i, 0))] + w_specs
                 + [pl.BlockSpec((1, tn), lambda i, j: (0, j))],
        out_specs=[out_spec] * (N_GROUPS + 2),
        out_shape=[jax.ShapeDtypeStruct((n, D_MODEL), act_dtype)] * N_GROUPS
                  + [jax.ShapeDtypeStruct((n, D_MODEL), F32)] * 2,
        scratch_shapes=[pltpu.VMEM((tm, D_MODEL), BF16)],
        compiler_params=_params("parallel", "arbitrary"),
        name="input_projection",
    )(x, *([w_in] * N_GROUPS), qkr_scale)


def _attn_kernel(slopes_ref, q_ref, k_ref, v_ref, lp_ref, g_ref, o_ref,
                 qs_ref, bias_ref, m_ref, l_ref, acc_ref, *, tq, lam_init):
    h = pl.program_id(1)
    i = pl.program_id(2)
    slope = slopes_ref[h]

    q = q_ref[...] * (DA_HEAD_DIM ** -0.5)
    lane = lax.broadcasted_iota(jnp.int32, q.shape, 1)
    zero = jnp.zeros_like(q)
    qs_ref[0:tq, :] = jnp.where(lane < DA_HEAD_DIM, q, zero)
    qs_ref[tq:2 * tq, :] = jnp.where(lane >= DA_HEAD_DIM, q, zero)

    row = lax.broadcasted_iota(jnp.int32, (tq, tq), 0)
    col = lax.broadcasted_iota(jnp.int32, (tq, tq), 1)
    rel = (col - row).astype(F32) * slope
    bias_ref[0:tq, :] = rel
    bias_ref[tq:2 * tq, :] = rel

    m_ref[...] = jnp.full_like(m_ref, NEG_BIG)
    l_ref[...] = jnp.zeros_like(l_ref)
    acc_ref[...] = jnp.zeros_like(acc_ref)

    def block(j, masked):
        k = k_ref[pl.ds(j * tq, tq), :]
        v = v_ref[pl.ds(j * tq, tq), :]
        s = lax.dot_general(qs_ref[...], k, _NT, preferred_element_type=F32)
        s = s + bias_ref[...] + slope * ((j - i) * tq).astype(F32)
        if masked:
            row2 = lax.broadcasted_iota(jnp.int32, (2 * tq, tq), 0) & (tq - 1)
            col2 = lax.broadcasted_iota(jnp.int32, (2 * tq, tq), 1)
            s = jnp.where(col2 <= row2, s, NEG_BIG)
        m_prev = m_ref[...]
        m_new = jnp.maximum(m_prev, jnp.max(s, axis=-1, keepdims=True))
        alpha = jnp.exp(m_prev - m_new)
        p = jnp.exp(s - m_new)
        l_ref[...] = alpha * l_ref[...] + jnp.sum(p, axis=-1, keepdims=True)
        acc_ref[...] = alpha * acc_ref[...] + jnp.dot(p.astype(BF16), v, preferred_element_type=F32)
        m_ref[...] = m_new

    def body(j, carry):
        block(j, masked=False)
        return carry

    lax.fori_loop(0, i, body, 0)
    block(i, masked=True)

    o = acc_ref[...] / l_ref[...]
    lam = _diff_lambda(lp_ref, lam_init)
    d = o[0:tq] - lam * o[tq:2 * tq]
    o_ref[...] = (_rms_norm(d) * g_ref[...] * (1.0 - lam_init)).astype(o_ref.dtype)


def _prompt_attention(q, k, v, lp, subln_g, *, lam_init, tq):
    bsz, seq, _ = q.shape
    assert tq & (tq - 1) == 0 and seq % tq == 0
    return pl.pallas_call(
        functools.partial(_attn_kernel, tq=tq, lam_init=lam_init),
        grid=(bsz, DA_HEADS, seq // tq),
        in_specs=[
            pl.BlockSpec(memory_space=pltpu.SMEM),
            pl.BlockSpec((None, tq, DA_V_DIM), lambda b, h, i: (b, i, h)),
            pl.BlockSpec((None, seq, DA_V_DIM), lambda b, h, i: (b, 0, h)),
            pl.BlockSpec((None, seq, DA_V_DIM), lambda b, h, i: (b, 0, h)),
            pl.BlockSpec((4, DA_HEAD_DIM), lambda b, h, i: (0, 0)),
            pl.BlockSpec((1, DA_V_DIM), lambda b, h, i: (0, 0)),
        ],
        out_specs=pl.BlockSpec((None, tq, DA_V_DIM), lambda b, h, i: (b, i, h)),
        out_shape=jax.ShapeDtypeStruct((bsz, seq, DA_HEADS * DA_V_DIM), BF16),
        scratch_shapes=[
            pltpu.VMEM((2 * tq, DA_V_DIM), BF16),
            pltpu.VMEM((2 * tq, tq), F32),
            pltpu.VMEM((2 * tq, 1), F32),
            pltpu.VMEM((2 * tq, 1), F32),
            pltpu.VMEM((2 * tq, DA_V_DIM), F32),
        ],
        compiler_params=_params("parallel", "parallel", "arbitrary"),
        name="prompt_diff_attention",
    )(jnp.asarray(_alibi_slopes()), q, k, v, lp, subln_g)


def _ret_kernel(logg_ref, q_ref, k_ref, v_ref, gr_ref, o_ref, sfin_ref, state_ref, *, chunk):
    h = pl.program_id(1)
    c = pl.program_id(2)
    lg = logg_ref[h]

    @pl.when(c == 0)
    def _():
        state_ref[...] = jnp.zeros_like(state_ref)

    q = q_ref[...]
    k = k_ref[...]
    v = v_ref[...]
    row = lax.broadcasted_iota(jnp.int32, (chunk, chunk), 0)
    col = lax.broadcasted_iota(jnp.int32, (chunk, chunk), 1)
    diff = (row - col).astype(F32)
    decay = jnp.where(diff >= 0, jnp.exp(jnp.maximum(diff, 0.0) * lg), 0.0)
    scores = lax.dot_general(q, k, _NT, preferred_element_type=F32) * decay
    idx = lax.broadcasted_iota(jnp.int32, (chunk, 1), 0).astype(F32)
    cross_decay = jnp.exp((idx + 1.0) * lg)
    state = state_ref[...]
    o = (jnp.dot(scores.astype(BF16), v, preferred_element_type=F32)
         + jnp.dot(q, state.astype(BF16), preferred_element_type=F32) * cross_decay)
    k_w = (k.astype(F32) * jnp.exp((chunk - 1.0 - idx) * lg)).astype(BF16)
    chunk_decay = jnp.exp(jnp.full((1, RET_V_DIM), chunk, F32) * lg)
    new_state = chunk_decay * state + lax.dot_general(
        k_w, v, (((0,), (0,)), ((), ())), preferred_element_type=F32)
    state_ref[...] = new_state
    o_ref[...] = (_rms_norm(o) * _silu(gr_ref[...].astype(F32))).astype(o_ref.dtype)

    @pl.when(c == pl.num_programs(2) - 1)
    def _():
        sfin_ref[...] = new_state


def _prompt_retention(qk, v, gr, *, chunk):
    bsz, seq, _ = qk.shape
    return pl.pallas_call(
        functools.partial(_ret_kernel, chunk=chunk),
        grid=(bsz, RET_HEADS, seq // chunk),
        in_specs=[
            pl.BlockSpec(memory_space=pltpu.SMEM),
            pl.BlockSpec((None, chunk, RET_K_DIM), lambda b, h, c: (b, c, h)),
            pl.BlockSpec((None, chunk, RET_K_DIM), lambda b, h, c: (b, c, RET_HEADS + h)),
            pl.BlockSpec((None, chunk, RET_V_DIM), lambda b, h, c: (b, c, h)),
            pl.BlockSpec((None, chunk, RET_V_DIM), lambda b, h, c: (b, c, h)),
        ],
        out_specs=[
            pl.BlockSpec((None, chunk, RET_V_DIM), lambda b, h, c: (b, c, h)),
            pl.BlockSpec((None, None, RET_K_DIM, RET_V_DIM), lambda b, h, c: (b, h, 0, 0)),
        ],
        out_shape=[
            jax.ShapeDtypeStruct((bsz, seq, RET_HEADS * RET_V_DIM), BF16),
            jax.ShapeDtypeStruct((bsz, RET_HEADS, RET_K_DIM, RET_V_DIM), F32),
        ],
        scratch_shapes=[pltpu.VMEM((RET_K_DIM, RET_V_DIM), F32)],
        compiler_params=_params("parallel", "parallel", "arbitrary"),
        name="prompt_retention",
    )(jnp.asarray(_retention_log_decay()), qk, qk, v, gr)


def _merge_kernel(x_ref, oa_ref, ob_ref, ga_ref, gb_ref, wa_ref, wb_ref, wo_ref, g_ref, b_ref, o_ref, *, alpha):
    a = jnp.dot(oa_ref[...].astype(BF16), wa_ref[...], preferred_element_type=F32)
    b = jnp.dot(ob_ref[...].astype(BF16), wb_ref[...], preferred_element_type=F32)
    m = jax.nn.sigmoid(ga_ref[...].astype(F32)) * a + jax.nn.sigmoid(gb_ref[...].astype(F32)) * b
    y = alpha * x_ref[...] + jnp.dot(m.astype(BF16), wo_ref[...], preferred_element_type=F32)
    o_ref[...] = _layer_norm(y, g_ref[...], b_ref[...])


def _merge_sublayer(x, oa, ob, ga, gb, w_a, w_b, w_o, g, b, *, alpha, tm):
    n = x.shape[0]
    row_spec = pl.BlockSpec((tm, D_MODEL), lambda i: (i, 0))
    w_spec = pl.BlockSpec((D_MODEL, D_MODEL), lambda i: (0, 0))
    vec_spec = pl.BlockSpec((1, D_MODEL), lambda i: (0, 0))
    return pl.pallas_call(
        functools.partial(_merge_kernel, alpha=alpha),
        grid=(n // tm,),
        in_specs=[row_spec] * 5 + [w_spec] * 3 + [vec_spec] * 2,
        out_specs=row_spec,
        out_shape=jax.ShapeDtypeStruct((n, D_MODEL), F32),
        compiler_params=_params("parallel"),
        name="merge_sublayer",
    )(x, oa, ob, ga, gb, w_a, w_b, w_o, g, b)


def _dec_attn_kernel(pt_ref, q_ref, slope_ref, k_ref, v_ref, kn_ref, vn_ref, lp_ref, g_ref, o_ref,
                     mb_ref, mbn_ref, m_ref, l_ref, acc_ref, *, t_dec, lam_init):
    del pt_ref
    p = pl.program_id(1)
    n_rows = 2 * t_dec * DA_HEADS
    page_rows = PAGE_SIZE * DA_HEADS

    @pl.when(p == 0)
    def _():
        m_ref[...] = jnp.full_like(m_ref, NEG_BIG)
        l_ref[...] = jnp.zeros_like(l_ref)
        acc_ref[...] = jnp.zeros_like(acc_ref)
        row = lax.broadcasted_iota(jnp.int32, (n_rows, page_rows), 0)
        col = lax.broadcasted_iota(jnp.int32, (n_rows, page_rows), 1)
        same_head = ((col - row) & (DA_HEADS - 1)) == 0
        tok = (row // DA_HEADS) % t_dec
        rel = (col // DA_HEADS - tok - PAST_LEN).astype(F32)
        mb_ref[...] = jnp.where(same_head, slope_ref[...] * rel, NEG_BIG)
        rown = lax.broadcasted_iota(jnp.int32, (n_rows, PAGE_SIZE), 0)
        coln = lax.broadcasted_iota(jnp.int32, (n_rows, PAGE_SIZE), 1)
        tokq = (rown // DA_HEADS) % t_dec
        tokk = coln // DA_HEADS
        ok = (((coln - rown) & (DA_HEADS - 1)) == 0) & (tokk <= tokq)
        mbn_ref[...] = jnp.where(ok, slope_ref[...] * (tokk - tokq).astype(F32), NEG_BIG)

    def update(kf, vf, maskbias):
        s = lax.dot_general(q_ref[...], kf.astype(BF16), _NT, preferred_element_type=F32) + maskbias
        m_prev = m_ref[...]
        m_new = jnp.maximum(m_prev, jnp.max(s, axis=-1, keepdims=True))
        alpha = jnp.exp(m_prev - m_new)
        pr = jnp.exp(s - m_new)
        l_ref[...] = alpha * l_ref[...] + jnp.sum(pr, axis=-1, keepdims=True)
        acc_ref[...] = alpha * acc_ref[...] + jnp.dot(pr.astype(BF16), vf.astype(BF16),
                                                      preferred_element_type=F32)
        m_ref[...] = m_new

    page_off = slope_ref[...] * (p * PAGE_SIZE).astype(F32)
    update(k_ref[...].reshape(page_rows, DA_V_DIM), v_ref[...].reshape(page_rows, DA_V_DIM),
           mb_ref[...] + page_off)

    @pl.when(p == pl.num_programs(1) - 1)
    def _():
        update(kn_ref[...], vn_ref[...], mbn_ref[...])
        o = acc_ref[...] / l_ref[...]
        lam = _diff_lambda(lp_ref, lam_init)
        half = n_rows // 2
        d = o[0:half] - lam * o[half:n_rows]
        o_ref[...] = _rms_norm(d) * g_ref[...] * (1.0 - lam_init)


def _sample_attention(page_table, q_rows, cache_k, cache_v, k_new, v_new, lp, subln_g, *, t_dec, lam_init):
    n_dec, n_pages = page_table.shape
    n_rows = 2 * t_dec * DA_HEADS
    assert t_dec * DA_HEADS <= PAGE_SIZE
    slope_rows = jnp.asarray(np.tile(_alibi_slopes(), 2 * t_dec).reshape(n_rows, 1))
    page_spec = pl.BlockSpec((None, PAGE_SIZE, DA_HEADS, DA_V_DIM), lambda s, p, pt: (pt[s, p], 0, 0, 0))
    new_spec = pl.BlockSpec((None, PAGE_SIZE, DA_V_DIM), lambda s, p, pt: (s, 0, 0))
    return pl.pallas_call(
        functools.partial(_dec_attn_kernel, t_dec=t_dec, lam_init=lam_init),
        grid_spec=pltpu.PrefetchScalarGridSpec(
            num_scalar_prefetch=1,
            grid=(n_dec, n_pages),
            in_specs=[
                pl.BlockSpec((None, n_rows, DA_V_DIM), lambda s, p, pt: (s, 0, 0)),
                pl.BlockSpec((n_rows, 1), lambda s, p, pt: (0, 0)),
                page_spec, page_spec, new_spec, new_spec,
                pl.BlockSpec((4, DA_HEAD_DIM), lambda s, p, pt: (0, 0)),
                pl.BlockSpec((1, DA_V_DIM), lambda s, p, pt: (0, 0)),
            ],
            out_specs=pl.BlockSpec((None, n_rows // 2, DA_V_DIM), lambda s, p, pt: (s, 0, 0)),
            scratch_shapes=[
                pltpu.VMEM((n_rows, PAGE_SIZE * DA_HEADS), F32),
                pltpu.VMEM((n_rows, PAGE_SIZE), F32),
                pltpu.VMEM((n_rows, 1), F32),
                pltpu.VMEM((n_rows, 1), F32),
                pltpu.VMEM((n_rows, DA_V_DIM), F32),
            ],
        ),
        out_shape=jax.ShapeDtypeStruct((n_dec, n_rows // 2, DA_V_DIM), F32),
        compiler_params=_params("parallel", "arbitrary"),
        name="sample_diff_attention",
    )(page_table, q_rows, slope_rows, cache_k, cache_v, k_new, v_new, lp, subln_g)


def _dec_ret_kernel(qk_ref, kt_ref, v_ref, gr_ref, st_ref, o_ref, snew_ref, *, t_dec, t_pad):
    log_g = _retention_log_decay()
    qk_all = qk_ref[...]
    v_all = v_ref[...]
    gr_all = gr_ref[...]
    tok = lax.broadcasted_iota(jnp.int32, (t_pad, 1), 0).astype(F32)
    for h in range(RET_HEADS):
        lg = float(log_g[h])
        q = qk_all[:, h * RET_K_DIM:(h + 1) * RET_K_DIM]
        k = qk_all[:, (RET_HEADS + h) * RET_K_DIM:(RET_HEADS + h + 1) * RET_K_DIM]
        v = v_all[:, h * RET_V_DIM:(h + 1) * RET_V_DIM]
        gr = gr_all[:, h * RET_V_DIM:(h + 1) * RET_V_DIM]
        kt = kt_ref[h]
        state = st_ref[h]
        o = jnp.dot(q.astype(BF16), state.astype(BF16), preferred_element_type=F32) * jnp.exp((tok + 1.0) * lg)
        new_state = math.exp(t_dec * lg) * state
        for j in range(t_dec):
            kj = kt[:, j:j + 1]
            vj = v[j:j + 1, :]
            s_j = jnp.sum(q * k[j:j + 1, :], axis=-1, keepdims=True)
            d_j = jnp.where(tok >= j, jnp.exp(jnp.maximum(tok - j, 0.0) * lg), 0.0)
            o = o + (s_j * d_j) * vj
            new_state = new_state + (kj * math.exp((t_dec - 1.0 - j) * lg)) * vj
        snew_ref[h] = new_state
        o_ref[:, h * RET_V_DIM:(h + 1) * RET_V_DIM] = _rms_norm(o) * _silu(gr)


def _sample_retention(qk, kt, v, gr, state, *, t_dec):
    n_dec, t_pad, _ = qk.shape
    return pl.pallas_call(
        functools.partial(_dec_ret_kernel, t_dec=t_dec, t_pad=t_pad),
        grid=(n_dec,),
        in_specs=[
            pl.BlockSpec((None, t_pad, 2 * RET_HEADS * RET_K_DIM), lambda s: (s, 0, 0)),
            pl.BlockSpec((None, RET_HEADS, RET_K_DIM, t_pad), lambda s: (s, 0, 0, 0)),
            pl.BlockSpec((None, t_pad, RET_HEADS * RET_V_DIM), lambda s: (s, 0, 0)),
            pl.BlockSpec((None, t_pad, RET_HEADS * RET_V_DIM), lambda s: (s, 0, 0)),
            pl.BlockSpec((None, RET_HEADS, RET_K_DIM, RET_V_DIM), lambda s: (s, 0, 0, 0)),
        ],
        out_specs=[
            pl.BlockSpec((None, t_pad, RET_HEADS * RET_V_DIM), lambda s: (s, 0, 0)),
            pl.BlockSpec((None, RET_HEADS, RET_K_DIM, RET_V_DIM), lambda s: (s, 0, 0, 0)),
        ],
        out_shape=[
            jax.ShapeDtypeStruct((n_dec, t_pad, RET_HEADS * RET_V_DIM), F32),
            jax.ShapeDtypeStruct(state.shape, F32),
        ],
        compiler_params=_params("parallel"),
        name="sample_retention",
    )(qk, kt, v, gr, state)


def _sample_attention_inputs(q, k, v, n_dec, t_dec):
    qh = q.reshape(n_dec, t_dec, DA_HEADS, DA_V_DIM) * (DA_HEAD_DIM ** -0.5)
    first = jnp.arange(DA_V_DIM) < DA_HEAD_DIM
    q_rows = jnp.stack([jnp.where(first, qh, 0.0), jnp.where(first, 0.0, qh)], axis=1)
    q_rows = q_rows.reshape(n_dec, 2 * t_dec * DA_HEADS, DA_V_DIM).astype(BF16)
    pad = ((0, 0), (0, PAGE_SIZE - t_dec * DA_HEADS), (0, 0))
    k_new = jnp.pad(k.reshape(n_dec, t_dec * DA_HEADS, DA_V_DIM), pad)
    v_new = jnp.pad(v.reshape(n_dec, t_dec * DA_HEADS, DA_V_DIM), pad)
    return q_rows, k_new, v_new


def kernel(x_prompt, x_sample, cache_k, cache_v, state_ret, page_table, ln_g, ln_b, ffn1_w_gu, ffn1_w_down,
           w_in, da_lambda, da_subln_g, w_branch_a, w_branch_b, w_out, ffn2_w_gu, ffn2_w_down):
    depth = w_in.shape[0]
    bsz, seq, _ = x_prompt.shape
    n_dec, t_dec, _ = x_sample.shape
    alpha = (2.0 * depth) ** 0.25
    t_pad = 8
    tm_p, tm_s = 512, n_dec * t_dec

    xp = x_prompt.reshape(bsz * seq, D_MODEL)
    xs = x_sample.reshape(n_dec * t_dec, D_MODEL)
    qkr_scale = jnp.concatenate([jnp.ones((RET_HEADS * RET_K_DIM,), F32),
                                 jnp.full((RET_HEADS * RET_K_DIM,), RET_K_DIM ** -0.5, F32)]).reshape(1, D_MODEL)
    outs = [[] for _ in range(6)]
    for l in range(depth):
        lam_init = _lambda_init(l)
        w1_gu, w1_d = ffn1_w_gu[l].astype(BF16), ffn1_w_down[l].astype(BF16)
        w2_gu, w2_d = ffn2_w_gu[l].astype(BF16), ffn2_w_down[l].astype(BF16)
        w_in_l = w_in[l].astype(BF16)
        w_a, w_b, w_o = (w[l].astype(BF16) for w in (w_branch_a, w_branch_b, w_out))
        ln = [(ln_g[l, s].reshape(1, D_MODEL), ln_b[l, s].reshape(1, D_MODEL)) for s in range(3)]
        lp = da_lambda[l].astype(F32)
        subln = da_subln_g[l].reshape(1, DA_V_DIM)

        xp = _ffn_sublayer(xp, w1_gu, w1_d, *ln[0], alpha=alpha, tm=tm_p, tf=256)
        proj = _input_projection(xp, w_in_l, qkr_scale, act_dtype=BF16, tm=tm_p, tn=256)
        k32, v32 = proj[N_GROUPS], proj[N_GROUPS + 1]
        as_seq = lambda t: t.reshape(bsz, seq, D_MODEL)
        oa = _prompt_attention(as_seq(proj[G_QA]), as_seq(proj[G_KA]), as_seq(proj[G_VA]), lp, subln,
                               lam_init=lam_init, tq=256)
        ob, s_fin = _prompt_retention(as_seq(proj[G_QKR]), as_seq(proj[G_VR]), as_seq(proj[G_GR]), chunk=128)
        xp = _merge_sublayer(xp, oa.reshape(bsz * seq, D_MODEL), ob.reshape(bsz * seq, D_MODEL),
                             proj[G_GA], proj[G_GB], w_a, w_b, w_o, *ln[1], alpha=alpha, tm=tm_p)
        xp = _ffn_sublayer(xp, w2_gu, w2_d, *ln[2], alpha=alpha, tm=tm_p, tf=256)
        outs[0].append(k32.reshape(bsz, seq, DA_HEADS, DA_V_DIM))
        outs[1].append(v32.reshape(bsz, seq, DA_HEADS, DA_V_DIM))
        outs[2].append(s_fin)

        xs = _ffn_sublayer(xs, w1_gu, w1_d, *ln[0], alpha=alpha, tm=tm_s, tf=256)
        proj = _input_projection(xs, w_in_l, qkr_scale, act_dtype=F32, tm=tm_s, tn=256)
        k32, v32 = proj[N_GROUPS], proj[N_GROUPS + 1]
        q_rows, k_new, v_new = _sample_attention_inputs(proj[G_QA], k32, v32, n_dec, t_dec)
        oa = _sample_attention(page_table, q_rows, cache_k[l], cache_v[l], k_new, v_new, lp, subln,
                               t_dec=t_dec, lam_init=lam_init)
        tok_pad = lambda t: jnp.pad(t.reshape(n_dec, t_dec, -1), ((0, 0), (0, t_pad - t_dec), (0, 0)))
        qk_r = tok_pad(proj[G_QKR])
        kt_r = qk_r[:, :, RET_HEADS * RET_K_DIM:].reshape(n_dec, t_pad, RET_HEADS, RET_K_DIM)
        kt_r = kt_r.transpose(0, 2, 3, 1)
        ob, s_new = _sample_retention(qk_r, kt_r, tok_pad(proj[G_VR]), tok_pad(proj[G_GR]), state_ret[l],
                                      t_dec=t_dec)
        ob = ob[:, :t_dec].reshape(n_dec * t_dec, D_MODEL)
        xs = _merge_sublayer(xs, oa.reshape(n_dec * t_dec, D_MODEL), ob, proj[G_GA], proj[G_GB],
                             w_a, w_b, w_o, *ln[1], alpha=alpha, tm=tm_s)
        xs = _ffn_sublayer(xs, w2_gu, w2_d, *ln[2], alpha=alpha, tm=tm_s, tf=256)
        outs[3].append(k32.reshape(n_dec, t_dec, DA_HEADS, DA_V_DIM))
        outs[4].append(v32.reshape(n_dec, t_dec, DA_HEADS, DA_V_DIM))
        outs[5].append(s_new)

    k_p, v_p, s_p, k_s, v_s, s_s = (jnp.stack(o) for o in outs)
    return (xp.reshape(bsz, seq, D_MODEL), xs.reshape(n_dec, t_dec, D_MODEL), k_p, v_p, s_p, k_s, v_s, s_s)
```

```python
import functools
import math

import numpy as np
import jax
import jax.numpy as jnp
from jax import lax
from jax.experimental import pallas as pl
from jax.experimental.pallas import tpu as pltpu

D_MODEL = 1024
PAST_LEN = 8192
PAGE_SIZE = 128
DA_HEADS = 8
DA_HEAD_DIM = 64
DA_V_DIM = 2 * DA_HEAD_DIM
RET_HEADS = 4
RET_K_DIM = 128
RET_V_DIM = 256
D_FF = 2816
LN_EPS = 1e-5
N_GROUPS = 8
G_QA, G_KA, G_VA, G_QKR, G_VR, G_GR, G_GA, G_GB = range(N_GROUPS)

VMEM_LIMIT_BYTES = 48 * 1024 * 1024
NEG_BIG = -1e30
ONES_ROWS = 16

F32 = jnp.float32
BF16 = jnp.bfloat16
_NT = (((1,), (1,)), ((), ()))


def _alibi_slopes():
    return (2.0 ** (-8.0 * np.arange(1, DA_HEADS + 1) / DA_HEADS)).astype(np.float32)


def _retention_log_decay():
    return np.log(1.0 - 2.0 ** (-5.0 - np.arange(RET_HEADS))).astype(np.float32)


def _lambda_init(layer):
    return 0.8 - 0.6 * math.exp(-0.3 * layer)


def _params(*semantics):
    return pltpu.CompilerParams(dimension_semantics=semantics, vmem_limit_bytes=VMEM_LIMIT_BYTES)


def _layer_norm(y, g, b):
    mu = jnp.mean(y, axis=-1, keepdims=True)
    d = y - mu
    var = jnp.mean(d * d, axis=-1, keepdims=True)
    return d * lax.rsqrt(var + LN_EPS) * g + b


def _rms_norm(o):
    return o * lax.rsqrt(jnp.mean(o * o, axis=-1, keepdims=True) + LN_EPS)


def _silu(a):
    return a * jax.nn.sigmoid(a)


def _diff_lambda(lp_ref, lam_init):
    lp = lp_ref[...]
    a = jnp.sum(lp[0:1] * lp[1:2], axis=-1, keepdims=True)
    b = jnp.sum(lp[2:3] * lp[3:4], axis=-1, keepdims=True)
    return jnp.exp(a) - jnp.exp(b) + lam_init


def _ffn_kernel(x_ref, wg_ref, wu_ref, wd_ref, g_ref, b_ref, o_ref, xb_ref, acc_ref, *, alpha):
    f = pl.program_id(1)

    @pl.when(f == 0)
    def _():
        xb_ref[...] = x_ref[...].astype(BF16)
        acc_ref[...] = jnp.zeros_like(acc_ref)

    xb = xb_ref[...]
    gate = jnp.dot(xb, wg_ref[...], preferred_element_type=F32)
    up = jnp.dot(xb, wu_ref[...], preferred_element_type=F32)
    h = (_silu(gate) * up).astype(BF16)
    acc_ref[...] += jnp.dot(h, wd_ref[...], preferred_element_type=F32)

    @pl.when(f == pl.num_programs(1) - 1)
    def _():
        y = alpha * x_ref[...] + 0.5 * acc_ref[...]
        o_ref[...] = _layer_norm(y, g_ref[...], b_ref[...])


def _ffn_sublayer(x, w_gu, w_down, g, b, *, alpha, tm, tf):
    n = x.shape[0]
    nf = D_FF // tf
    return pl.pallas_call(
        functools.partial(_ffn_kernel, alpha=alpha),
        grid=(n // tm, nf),
        in_specs=[
            pl.BlockSpec((tm, D_MODEL), lambda i, f: (i, 0)),
            pl.BlockSpec((D_MODEL, tf), lambda i, f: (0, f)),
            pl.BlockSpec((D_MODEL, tf), lambda i, f: (0, f + nf)),
            pl.BlockSpec((tf, D_MODEL), lambda i, f: (f, 0)),
            pl.BlockSpec((1, D_MODEL), lambda i, f: (0, 0)),
            pl.BlockSpec((1, D_MODEL), lambda i, f: (0, 0)),
        ],
        out_specs=pl.BlockSpec((tm, D_MODEL), lambda i, f: (i, 0)),
        out_shape=jax.ShapeDtypeStruct((n, D_MODEL), F32),
        scratch_shapes=[pltpu.VMEM((tm, D_MODEL), BF16), pltpu.VMEM((tm, D_MODEL), F32)],
        compiler_params=_params("parallel", "arbitrary"),
        name="ffn_sublayer",
    )(x, w_gu, w_gu, w_down, g, b)


def _proj_kernel(x_ref, *refs):
    w_refs = refs[:N_GROUPS]
    scale_ref = refs[N_GROUPS]
    out_refs = refs[N_GROUPS + 1:2 * N_GROUPS + 1]
    k32_ref, v32_ref, xb_ref = refs[2 * N_GROUPS + 1:]

    @pl.when(pl.program_id(1) == 0)
    def _():
        xb_ref[...] = x_ref[...].astype(BF16)

    xb = xb_ref[...]
    for grp in range(N_GROUPS):
        y = jnp.dot(xb, w_refs[grp][...], preferred_element_type=F32)
        if grp == G_QKR:
            y = y * scale_ref[...]
        out_refs[grp][...] = y.astype(out_refs[grp].dtype)
        if grp == G_KA:
            k32_ref[...] = y
        if grp == G_VA:
            v32_ref[...] = y


def _input_projection(x, w_in, qkr_scale, *, act_dtype, tm, tn):
    n = x.shape[0]
    nj = D_MODEL // tn
    w_specs = [pl.BlockSpec((D_MODEL, tn), functools.partial(lambda i, j, grp: (0, grp * nj + j), grp=grp))
               for grp in range(N_GROUPS)]
    out_spec = pl.BlockSpec((tm, tn), lambda i, j: (i, j))
    return pl.pallas_call(
        _proj_kernel,
        grid=(n // tm, nj),
        in_specs=[pl.BlockSpec((tm, D_MODEL), lambda i, j: (i, 0))] + w_specs
                 + [pl.BlockSpec((1, tn), lambda i, j: (0, j))],
        out_specs=[out_spec] * (N_GROUPS + 2),
        out_shape=[jax.ShapeDtypeStruct((n, D_MODEL), act_dtype)] * N_GROUPS
                  + [jax.ShapeDtypeStruct((n, D_MODEL), F32)] * 2,
        scratch_shapes=[pltpu.VMEM((tm, D_MODEL), BF16)],
        compiler_params=_params("parallel", "arbitrary"),
        name="input_projection",
    )(x, *([w_in] * N_GROUPS), qkr_scale)


def _attn_kernel(slopes_ref, q_ref, k_ref, vt_ref, lp_ref, g_ref, o_ref,
                 qst_ref, bias_ref, m_ref, acc_ref, *, tq, lam_init):
    h = pl.program_id(1)
    i = pl.program_id(2)
    slope = slopes_ref[h]

    qt = (q_ref[...].astype(F32) * (DA_HEAD_DIM ** -0.5)).T.astype(BF16)
    sub = lax.broadcasted_iota(jnp.int32, qt.shape, 0)
    zero = jnp.zeros_like(qt)
    qst_ref[:, 0:tq] = jnp.where(sub < DA_HEAD_DIM, qt, zero)
    qst_ref[:, tq:2 * tq] = jnp.where(sub >= DA_HEAD_DIM, qt, zero)

    key = lax.broadcasted_iota(jnp.int32, (tq, tq), 0)
    qry = lax.broadcasted_iota(jnp.int32, (tq, tq), 1)
    rel = (key - qry).astype(F32) * slope
    bias_ref[:, 0:tq] = rel
    bias_ref[:, tq:2 * tq] = rel

    m_ref[...] = jnp.full_like(m_ref, NEG_BIG)
    acc_ref[...] = jnp.zeros_like(acc_ref)

    def step(js, masked):
        m_prev = m_ref[...]
        m_new = m_prev
        scores = []
        for n, j in enumerate(js):
            k = k_ref[pl.ds(j * tq, tq), :]
            s = jnp.dot(k, qst_ref[...], preferred_element_type=F32) + bias_ref[...]
            if masked and n == len(js) - 1:
                key2 = lax.broadcasted_iota(jnp.int32, (tq, 2 * tq), 0)
                qry2 = lax.broadcasted_iota(jnp.int32, (tq, 2 * tq), 1) & (tq - 1)
                s = jnp.where(key2 <= qry2, s, NEG_BIG)
            c = slope * ((j - i) * tq).astype(F32)
            m_new = jnp.maximum(m_new, jnp.max(s, axis=0, keepdims=True) + c)
            scores.append((s, c))
        acc = jnp.exp(m_prev - m_new) * acc_ref[...]
        for (s, c), j in zip(scores, js):
            p = jnp.exp(s - (m_new - c)).astype(BF16)
            acc = acc + jnp.dot(vt_ref[j], p, preferred_element_type=F32)
        acc_ref[...] = acc
        m_ref[...] = m_new

    def pair_body(t, carry):
        step([2 * t, 2 * t + 1], masked=False)
        return carry

    lax.fori_loop(0, i // 2, pair_body, 0)

    @pl.when(i % 2 == 1)
    def _():
        step([i - 1, i], masked=True)

    @pl.when(i % 2 == 0)
    def _():
        step([i], masked=True)

    o = acc_ref[0:DA_V_DIM, :] / acc_ref[DA_V_DIM:DA_V_DIM + 1, :]
    lam = _diff_lambda(lp_ref, lam_init)
    d = o[:, 0:tq] - lam * o[:, tq:2 * tq]
    r = d * lax.rsqrt(jnp.mean(d * d, axis=0, keepdims=True) + LN_EPS) * g_ref[...] * (1.0 - lam_init)
    o_ref[...] = r.T.astype(o_ref.dtype)


def _prompt_attention(q, k, vt, lp, subln_g_col, *, lam_init, tq):
    bsz, seq, _ = q.shape
    assert tq & (tq - 1) == 0 and seq % tq == 0
    nk = seq // tq
    vt_rows = DA_V_DIM + ONES_ROWS
    return pl.pallas_call(
        functools.partial(_attn_kernel, tq=tq, lam_init=lam_init),
        grid=(bsz, DA_HEADS, nk),
        in_specs=[
            pl.BlockSpec(memory_space=pltpu.SMEM),
            pl.BlockSpec((None, tq, DA_V_DIM), lambda b, h, i: (b, i, h)),
            pl.BlockSpec((None, seq, DA_V_DIM), lambda b, h, i: (b, 0, h)),
            pl.BlockSpec((None, None, nk, vt_rows, tq), lambda b, h, i: (b, h, 0, 0, 0)),
            pl.BlockSpec((4, DA_HEAD_DIM), lambda b, h, i: (0, 0)),
            pl.BlockSpec((DA_V_DIM, 1), lambda b, h, i: (0, 0)),
        ],
        out_specs=pl.BlockSpec((None, tq, DA_V_DIM), lambda b, h, i: (b, i, h)),
        out_shape=jax.ShapeDtypeStruct((bsz, seq, DA_HEADS * DA_V_DIM), BF16),
        scratch_shapes=[
            pltpu.VMEM((DA_V_DIM, 2 * tq), BF16),
            pltpu.VMEM((tq, 2 * tq), F32),
            pltpu.VMEM((1, 2 * tq), F32),
            pltpu.VMEM((vt_rows, 2 * tq), F32),
        ],
        compiler_params=_params("parallel", "parallel", "arbitrary"),
        name="prompt_diff_attention",
    )(jnp.asarray(_alibi_slopes()), q, k, vt, lp, subln_g_col)


def _ret_kernel(logg_ref, q_ref, k_ref, v_ref, gr_ref, o_ref, sfin_ref, state_ref, *, chunk):
    h = pl.program_id(1)
    c = pl.program_id(2)
    lg = logg_ref[h]

    @pl.when(c == 0)
    def _():
        state_ref[...] = jnp.zeros_like(state_ref)

    q = q_ref[...]
    k = k_ref[...]
    v = v_ref[...]
    row = lax.broadcasted_iota(jnp.int32, (chunk, chunk), 0)
    col = lax.broadcasted_iota(jnp.int32, (chunk, chunk), 1)
    diff = (row - col).astype(F32)
    decay = jnp.where(diff >= 0, jnp.exp(jnp.maximum(diff, 0.0) * lg), 0.0)
    scores = lax.dot_general(q, k, _NT, preferred_element_type=F32) * decay
    idx = lax.broadcasted_iota(jnp.int32, (chunk, 1), 0).astype(F32)
    cross_decay = jnp.exp((idx + 1.0) * lg)
    state = state_ref[...]
    o = (jnp.dot(scores.astype(BF16), v, preferred_element_type=F32)
         + jnp.dot(q, state.astype(BF16), preferred_element_type=F32) * cross_decay)
    k_w = (k.astype(F32) * jnp.exp((chunk - 1.0 - idx) * lg)).astype(BF16)
    chunk_decay = jnp.exp(jnp.full((1, RET_V_DIM), chunk, F32) * lg)
    new_state = chunk_decay * state + lax.dot_general(
        k_w, v, (((0,), (0,)), ((), ())), preferred_element_type=F32)
    state_ref[...] = new_state
    o_ref[...] = (_rms_norm(o) * _silu(gr_ref[...].astype(F32))).astype(o_ref.dtype)

    @pl.when(c == pl.num_programs(2) - 1)
    def _():
        sfin_ref[...] = new_state


def _prompt_retention(qk, v, gr, *, chunk):
    bsz, seq, _ = qk.shape
    return pl.pallas_call(
        functools.partial(_ret_kernel, chunk=chunk),
        grid=(bsz, RET_HEADS, seq // chunk),
        in_specs=[
            pl.BlockSpec(memory_space=pltpu.SMEM),
            pl.BlockSpec((None, chunk, RET_K_DIM), lambda b, h, c: (b, c, h)),
            pl.BlockSpec((None, chunk, RET_K_DIM), lambda b, h, c: (b, c, RET_HEADS + h)),
            pl.BlockSpec((None, chunk, RET_V_DIM), lambda b, h, c: (b, c, h)),
            pl.BlockSpec((None, chunk, RET_V_DIM), lambda b, h, c: (b, c, h)),
        ],
        out_specs=[
            pl.BlockSpec((None, chunk, RET_V_DIM), lambda b, h, c: (b, c, h)),
            pl.BlockSpec((None, None, RET_K_DIM, RET_V_DIM), lambda b, h, c: (b, h, 0, 0)),
        ],
        out_shape=[
            jax.ShapeDtypeStruct((bsz, seq, RET_HEADS * RET_V_DIM), BF16),
            jax.ShapeDtypeStruct((bsz, RET_HEADS, RET_K_DIM, RET_V_DIM), F32),
        ],
        scratch_shapes=[pltpu.VMEM((RET_K_DIM, RET_V_DIM), F32)],
        compiler_params=_params("parallel", "parallel", "arbitrary"),
        name="prompt_retention",
    )(jnp.asarray(_retention_log_decay()), qk, qk, v, gr)


def _merge_kernel(x_ref, oa_ref, ob_ref, ga_ref, gb_ref, wa_ref, wb_ref, wo_ref, g_ref, b_ref, o_ref, *, alpha):
    a = jnp.dot(oa_ref[...].astype(BF16), wa_ref[...], preferred_element_type=F32)
    b = jnp.dot(ob_ref[...].astype(BF16), wb_ref[...], preferred_element_type=F32)
    m = jax.nn.sigmoid(ga_ref[...].astype(F32)) * a + jax.nn.sigmoid(gb_ref[...].astype(F32)) * b
    y = alpha * x_ref[...] + jnp.dot(m.astype(BF16), wo_ref[...], preferred_element_type=F32)
    o_ref[...] = _layer_norm(y, g_ref[...], b_ref[...])


def _merge_sublayer(x, oa, ob, ga, gb, w_a, w_b, w_o, g, b, *, alpha, tm):
    n = x.shape[0]
    row_spec = pl.BlockSpec((tm, D_MODEL), lambda i: (i, 0))
    w_spec = pl.BlockSpec((D_MODEL, D_MODEL), lambda i: (0, 0))
    vec_spec = pl.BlockSpec((1, D_MODEL), lambda i: (0, 0))
    return pl.pallas_call(
        functools.partial(_merge_kernel, alpha=alpha),
        grid=(n // tm,),
        in_specs=[row_spec] * 5 + [w_spec] * 3 + [vec_spec] * 2,
        out_specs=row_spec,
        out_shape=jax.ShapeDtypeStruct((n, D_MODEL), F32),
        compiler_params=_params("parallel"),
        name="merge_sublayer",
    )(x, oa, ob, ga, gb, w_a, w_b, w_o, g, b)


def _dec_attn_kernel(pt_ref, q_ref, slope_ref, *refs, t_dec, lam_init, n_pp):
    del pt_ref
    k_refs, v_refs = refs[:n_pp], refs[n_pp:2 * n_pp]
    kn_ref, vn_ref, lp_ref, g_ref, o_ref, mb_ref, mbn_ref, m_ref, l_ref, acc_ref = refs[2 * n_pp:]
    p = pl.program_id(1)
    n_rows = 2 * t_dec * DA_HEADS
    page_rows = PAGE_SIZE * DA_HEADS

    @pl.when(p == 0)
    def _():
        m_ref[...] = jnp.full_like(m_ref, NEG_BIG)
        l_ref[...] = jnp.zeros_like(l_ref)
        acc_ref[...] = jnp.zeros_like(acc_ref)
        row = lax.broadcasted_iota(jnp.int32, (n_rows, page_rows), 0)
        col = lax.broadcasted_iota(jnp.int32, (n_rows, page_rows), 1)
        same_head = ((col - row) & (DA_HEADS - 1)) == 0
        tok = (row // DA_HEADS) % t_dec
        rel = (col // DA_HEADS - tok - PAST_LEN).astype(F32)
        mb_ref[...] = jnp.where(same_head, slope_ref[...] * rel, NEG_BIG)
        rown = lax.broadcasted_iota(jnp.int32, (n_rows, PAGE_SIZE), 0)
        coln = lax.broadcasted_iota(jnp.int32, (n_rows, PAGE_SIZE), 1)
        tokq = (rown // DA_HEADS) % t_dec
        tokk = coln // DA_HEADS
        ok = (((coln - rown) & (DA_HEADS - 1)) == 0) & (tokk <= tokq)
        mbn_ref[...] = jnp.where(ok, slope_ref[...] * (tokk - tokq).astype(F32), NEG_BIG)

    def update(blocks):
        q = q_ref[...]
        scores = [lax.dot_general(q, kf.astype(BF16), _NT, preferred_element_type=F32) + mb
                  for kf, _, mb in blocks]
        m_prev = m_ref[...]
        m_new = m_prev
        for s in scores:
            m_new = jnp.maximum(m_new, jnp.max(s, axis=-1, keepdims=True))
        alpha = jnp.exp(m_prev - m_new)
        l_new = alpha * l_ref[...]
        acc_new = alpha * acc_ref[...]
        for s, (_, vf, _) in zip(scores, blocks):
            pr = jnp.exp(s - m_new)
            l_new = l_new + jnp.sum(pr, axis=-1, keepdims=True)
            acc_new = acc_new + jnp.dot(pr.astype(BF16), vf.astype(BF16), preferred_element_type=F32)
        l_ref[...] = l_new
        acc_ref[...] = acc_new
        m_ref[...] = m_new

    slope = slope_ref[...]
    update([(k_refs[r][...].reshape(page_rows, DA_V_DIM), v_refs[r][...].reshape(page_rows, DA_V_DIM),
             mb_ref[...] + slope * ((p * n_pp + r) * PAGE_SIZE).astype(F32)) for r in range(n_pp)])

    @pl.when(p == pl.num_programs(1) - 1)
    def _():
        update([(kn_ref[...], vn_ref[...], mbn_ref[...])])
        o = acc_ref[...] / l_ref[...]
        lam = _diff_lambda(lp_ref, lam_init)
        half = n_rows // 2
        d = o[0:half] - lam * o[half:n_rows]
        o_ref[...] = _rms_norm(d) * g_ref[...] * (1.0 - lam_init)


def _sample_attention(page_table, q_rows, cache_k, cache_v, k_new, v_new, lp, subln_g, *,
                      t_dec, lam_init, n_pp):
    n_dec, n_pages = page_table.shape
    n_rows = 2 * t_dec * DA_HEADS
    assert t_dec * DA_HEADS <= PAGE_SIZE and n_pages % n_pp == 0
    slope_rows = jnp.asarray(np.tile(_alibi_slopes(), 2 * t_dec).reshape(n_rows, 1))
    page_specs = [pl.BlockSpec((None, PAGE_SIZE, DA_HEADS, DA_V_DIM),
                               functools.partial(lambda s, p, pt, r: (pt[s, p * n_pp + r], 0, 0, 0), r=r))
                  for r in range(n_pp)]
    new_spec = pl.BlockSpec((None, PAGE_SIZE, DA_V_DIM), lambda s, p, pt: (s, 0, 0))
    return pl.pallas_call(
        functools.partial(_dec_attn_kernel, t_dec=t_dec, lam_init=lam_init, n_pp=n_pp),
        grid_spec=pltpu.PrefetchScalarGridSpec(
            num_scalar_prefetch=1,
            grid=(n_dec, n_pages // n_pp),
            in_specs=[
                pl.BlockSpec((None, n_rows, DA_V_DIM), lambda s, p, pt: (s, 0, 0)),
                pl.BlockSpec((n_rows, 1), lambda s, p, pt: (0, 0)),
                *page_specs, *page_specs, new_spec, new_spec,
                pl.BlockSpec((4, DA_HEAD_DIM), lambda s, p, pt: (0, 0)),
                pl.BlockSpec((1, DA_V_DIM), lambda s, p, pt: (0, 0)),
            ],
            out_specs=pl.BlockSpec((None, n_rows // 2, DA_V_DIM), lambda s, p, pt: (s, 0, 0)),
            scratch_shapes=[
                pltpu.VMEM((n_rows, PAGE_SIZE * DA_HEADS), F32),
                pltpu.VMEM((n_rows, PAGE_SIZE), F32),
                pltpu.VMEM((n_rows, 1), F32),
                pltpu.VMEM((n_rows, 1), F32),
                pltpu.VMEM((n_rows, DA_V_DIM), F32),
            ],
        ),
        out_shape=jax.ShapeDtypeStruct((n_dec, n_rows // 2, DA_V_DIM), F32),
        compiler_params=_params("parallel", "arbitrary"),
        name="sample_diff_attention",
    )(page_table, q_rows, slope_rows, *([cache_k] * n_pp), *([cache_v] * n_pp), k_new, v_new, lp, subln_g)


def _dec_ret_kernel(qk_ref, kt_ref, v_ref, gr_ref, st_ref, o_ref, snew_ref, *, t_dec, t_pad):
    log_g = _retention_log_decay()
    qk_all = qk_ref[...]
    v_all = v_ref[...]
    gr_all = gr_ref[...]
    tok = lax.broadcasted_iota(jnp.int32, (t_pad, 1), 0).astype(F32)
    for h in range(RET_HEADS):
        lg = float(log_g[h])
        q = qk_all[:, h * RET_K_DIM:(h + 1) * RET_K_DIM]
        k = qk_all[:, (RET_HEADS + h) * RET_K_DIM:(RET_HEADS + h + 1) * RET_K_DIM]
        v = v_all[:, h * RET_V_DIM:(h + 1) * RET_V_DIM]
        gr = gr_all[:, h * RET_V_DIM:(h + 1) * RET_V_DIM]
        kt = kt_ref[h]
        state = st_ref[h]
        o = jnp.dot(q.astype(BF16), state.astype(BF16), preferred_element_type=F32) * jnp.exp((tok + 1.0) * lg)
        new_state = math.exp(t_dec * lg) * state
        for j in range(t_dec):
            kj = kt[:, j:j + 1]
            vj = v[j:j + 1, :]
            s_j = jnp.sum(q * k[j:j + 1, :], axis=-1, keepdims=True)
            d_j = jnp.where(tok >= j, jnp.exp(jnp.maximum(tok - j, 0.0) * lg), 0.0)
            o = o + (s_j * d_j) * vj
            new_state = new_state + (kj * math.exp((t_dec - 1.0 - j) * lg)) * vj
        snew_ref[h] = new_state
        o_ref[:, h * RET_V_DIM:(h + 1) * RET_V_DIM] = _rms_norm(o) * _silu(gr)


def _sample_retention(qk, kt, v, gr, state, *, t_dec):
    n_dec, t_pad, _ = qk.shape
    return pl.pallas_call(
        functools.partial(_dec_ret_kernel, t_dec=t_dec, t_pad=t_pad),
        grid=(n_dec,),
        in_specs=[
            pl.BlockSpec((None, t_pad, 2 * RET_HEADS * RET_K_DIM), lambda s: (s, 0, 0)),
            pl.BlockSpec((None, RET_HEADS, RET_K_DIM, t_pad), lambda s: (s, 0, 0, 0)),
            pl.BlockSpec((None, t_pad, RET_HEADS * RET_V_DIM), lambda s: (s, 0, 0)),
            pl.BlockSpec((None, t_pad, RET_HEADS * RET_V_DIM), lambda s: (s, 0, 0)),
            pl.BlockSpec((None, RET_HEADS, RET_K_DIM, RET_V_DIM), lambda s: (s, 0, 0, 0)),
        ],
        out_specs=[
            pl.BlockSpec((None, t_pad, RET_HEADS * RET_V_DIM), lambda s: (s, 0, 0)),
            pl.BlockSpec((None, RET_HEADS, RET_K_DIM, RET_V_DIM), lambda s: (s, 0, 0, 0)),
        ],
        out_shape=[
            jax.ShapeDtypeStruct((n_dec, t_pad, RET_HEADS * RET_V_DIM), F32),
            jax.ShapeDtypeStruct(state.shape, F32),
        ],
        compiler_params=_params("parallel"),
        name="sample_retention",
    )(qk, kt, v, gr, state)


def _sample_attention_inputs(q, k, v, n_dec, t_dec):
    qh = q.reshape(n_dec, t_dec, DA_HEADS, DA_V_DIM) * (DA_HEAD_DIM ** -0.5)
    first = jnp.arange(DA_V_DIM) < DA_HEAD_DIM
    q_rows = jnp.stack([jnp.where(first, qh, 0.0), jnp.where(first, 0.0, qh)], axis=1)
    q_rows = q_rows.reshape(n_dec, 2 * t_dec * DA_HEADS, DA_V_DIM).astype(BF16)
    pad = ((0, 0), (0, PAGE_SIZE - t_dec * DA_HEADS), (0, 0))
    k_new = jnp.pad(k.reshape(n_dec, t_dec * DA_HEADS, DA_V_DIM), pad)
    v_new = jnp.pad(v.reshape(n_dec, t_dec * DA_HEADS, DA_V_DIM), pad)
    return q_rows, k_new, v_new


def kernel(x_prompt, x_sample, cache_k, cache_v, state_ret, page_table, ln_g, ln_b, ffn1_w_gu, ffn1_w_down,
           w_in, da_lambda, da_subln_g, w_branch_a, w_branch_b, w_out, ffn2_w_gu, ffn2_w_down):
    depth = w_in.shape[0]
    bsz, seq, _ = x_prompt.shape
    n_dec, t_dec, _ = x_sample.shape
    alpha = (2.0 * depth) ** 0.25
    t_pad = 8
    tm_p, tm_s = 512, n_dec * t_dec
    tq = 256

    xp = x_prompt.reshape(bsz * seq, D_MODEL)
    xs = x_sample.reshape(n_dec * t_dec, D_MODEL)
    qkr_scale = jnp.concatenate([jnp.ones((RET_HEADS * RET_K_DIM,), F32),
                                 jnp.full((RET_HEADS * RET_K_DIM,), RET_K_DIM ** -0.5, F32)]).reshape(1, D_MODEL)
    outs = [[] for _ in range(6)]
    for l in range(depth):
        lam_init = _lambda_init(l)
        w1_gu, w1_d = ffn1_w_gu[l].astype(BF16), ffn1_w_down[l].astype(BF16)
        w2_gu, w2_d = ffn2_w_gu[l].astype(BF16), ffn2_w_down[l].astype(BF16)
        w_in_l = w_in[l].astype(BF16)
        w_a, w_b, w_o = (w[l].astype(BF16) for w in (w_branch_a, w_branch_b, w_out))
        ln = [(ln_g[l, s].reshape(1, D_MODEL), ln_b[l, s].reshape(1, D_MODEL)) for s in range(3)]
        lp = da_lambda[l].astype(F32)
        subln = da_subln_g[l].reshape(1, DA_V_DIM)

        xp = _ffn_sublayer(xp, w1_gu, w1_d, *ln[0], alpha=alpha, tm=tm_p, tf=256)
        proj = _input_projection(xp, w_in_l, qkr_scale, act_dtype=BF16, tm=tm_p, tn=256)
        k32, v32 = proj[N_GROUPS], proj[N_GROUPS + 1]
        as_seq = lambda t: t.reshape(bsz, seq, D_MODEL)
        vt = proj[G_VA].reshape(bsz, seq // tq, tq, DA_HEADS, DA_V_DIM).transpose(0, 3, 1, 4, 2)
        vt = jnp.concatenate([vt, jnp.ones((bsz, DA_HEADS, seq // tq, ONES_ROWS, tq), BF16)], axis=3)
        oa = _prompt_attention(as_seq(proj[G_QA]), as_seq(proj[G_KA]), vt, lp, subln.reshape(DA_V_DIM, 1),
                               lam_init=lam_init, tq=tq)
        ob, s_fin = _prompt_retention(as_seq(proj[G_QKR]), as_seq(proj[G_VR]), as_seq(proj[G_GR]), chunk=128)
        xp = _merge_sublayer(xp, oa.reshape(bsz * seq, D_MODEL), ob.reshape(bsz * seq, D_MODEL),
                             proj[G_GA], proj[G_GB], w_a, w_b, w_o, *ln[1], alpha=alpha, tm=tm_p)
        xp = _ffn_sublayer(xp, w2_gu, w2_d, *ln[2], alpha=alpha, tm=tm_p, tf=256)
        outs[0].append(k32.reshape(bsz, seq, DA_HEADS, DA_V_DIM))
        outs[1].append(v32.reshape(bsz, seq, DA_HEADS, DA_V_DIM))
        outs[2].append(s_fin)

        xs = _ffn_sublayer(xs, w1_gu, w1_d, *ln[0], alpha=alpha, tm=tm_s, tf=256)
        proj = _input_projection(xs, w_in_l, qkr_scale, act_dtype=F32, tm=tm_s, tn=256)
        k32, v32 = proj[N_GROUPS], proj[N_GROUPS + 1]
        q_rows, k_new, v_new = _sample_attention_inputs(proj[G_QA], k32, v32, n_dec, t_dec)
        oa = _sample_attention(page_table, q_rows, cache_k[l], cache_v[l], k_new, v_new, lp, subln,
                               t_dec=t_dec, lam_init=lam_init, n_pp=8)
        tok_pad = lambda t: jnp.pad(t.reshape(n_dec, t_dec, -1), ((0, 0), (0, t_pad - t_dec), (0, 0)))
        qk_r = tok_pad(proj[G_QKR])
        kt_r = qk_r[:, :, RET_HEADS * RET_K_DIM:].reshape(n_dec, t_pad, RET_HEADS, RET_K_DIM)
        kt_r = kt_r.transpose(0, 2, 3, 1)
        ob, s_new = _sample_retention(qk_r, kt_r, tok_pad(proj[G_VR]), tok_pad(proj[G_GR]), state_ret[l],
                                      t_dec=t_dec)
        ob = ob[:, :t_dec].reshape(n_dec * t_dec, D_MODEL)
        xs = _merge_sublayer(xs, oa.reshape(n_dec * t_dec, D_MODEL), ob, proj[G_GA], proj[G_GB],
                             w_a, w_b, w_o, *ln[1], alpha=alpha, tm=tm_s)
        xs = _ffn_sublayer(xs, w2_gu, w2_d, *ln[2], alpha=alpha, tm=tm_s, tf=256)
        outs[3].append(k32.reshape(n_dec, t_dec, DA_HEADS, DA_V_DIM))
        outs[4].append(v32.reshape(n_dec, t_dec, DA_HEADS, DA_V_DIM))
        outs[5].append(s_new)

    k_p, v_p, s_p, k_s, v_s, s_s = (jnp.stack(o) for o in outs)
    return (xp.reshape(bsz, seq, D_MODEL), xs.reshape(n_dec, t_dec, D_MODEL), k_p, v_p, s_p, k_s, v_s, s_s)
```

```python
import functools
import math

import numpy as np
import jax
import jax.numpy as jnp
from jax import lax
from jax.experimental import pallas as pl
from jax.experimental.pallas import tpu as pltpu

D_MODEL = 1024
PAST_LEN = 8192
PAGE_SIZE = 128
DA_HEADS = 8
DA_HEAD_DIM = 64
DA_V_DIM = 2 * DA_HEAD_DIM
RET_HEADS = 4
RET_K_DIM = 128
RET_V_DIM = 256
D_FF = 2816
LN_EPS = 1e-5
N_GROUPS = 8
G_QA, G_KA, G_VA, G_QKR, G_VR, G_GR, G_GA, G_GB = range(N_GROUPS)

VMEM_LIMIT_BYTES = 48 * 1024 * 1024
NEG_BIG = -1e30
ONES_ROWS = 16
MXU_DIM = 256
FF_CHUNKS = tuple((s, min(3 * MXU_DIM, D_FF - s)) for s in range(0, D_FF, 3 * MXU_DIM))
assert all(size % MXU_DIM == 0 for _, size in FF_CHUNKS)

F32 = jnp.float32
BF16 = jnp.bfloat16
_NT = (((1,), (1,)), ((), ()))


def _alibi_slopes():
    return (2.0 ** (-8.0 * np.arange(1, DA_HEADS + 1) / DA_HEADS)).astype(np.float32)


def _retention_log_decay():
    return np.log(1.0 - 2.0 ** (-5.0 - np.arange(RET_HEADS))).astype(np.float32)


def _lambda_init(layer):
    return 0.8 - 0.6 * math.exp(-0.3 * layer)


def _params(*semantics):
    return pltpu.CompilerParams(dimension_semantics=semantics, vmem_limit_bytes=VMEM_LIMIT_BYTES)


def _layer_norm(y, g, b):
    mu = jnp.mean(y, axis=-1, keepdims=True)
    d = y - mu
    var = jnp.mean(d * d, axis=-1, keepdims=True)
    return d * lax.rsqrt(var + LN_EPS) * g + b


def _rms_norm(o):
    return o * lax.rsqrt(jnp.mean(o * o, axis=-1, keepdims=True) + LN_EPS)


def _silu(a):
    return a * jax.nn.sigmoid(a)


def _diff_lambda(lp_ref, lam_init):
    lp = lp_ref[...]
    a = jnp.sum(lp[0:1] * lp[1:2], axis=-1, keepdims=True)
    b = jnp.sum(lp[2:3] * lp[3:4], axis=-1, keepdims=True)
    return jnp.exp(a) - jnp.exp(b) + lam_init


def _ffn_kernel(x_ref, wgu_ref, wd_ref, g_ref, b_ref, o_ref, *, alpha):
    x = x_ref[...]
    xb = x.astype(BF16)
    acc = None
    for start, size in FF_CHUNKS:
        gate = jnp.dot(xb, wgu_ref[:, start:start + size], preferred_element_type=F32)
        up = jnp.dot(xb, wgu_ref[:, D_FF + start:D_FF + start + size], preferred_element_type=F32)
        h = (_silu(gate) * up).astype(BF16)
        part = jnp.dot(h, wd_ref[start:start + size, :], preferred_element_type=F32)
        acc = part if acc is None else acc + part
    o_ref[...] = _layer_norm(alpha * x + 0.5 * acc, g_ref[...], b_ref[...])


def _ffn_sublayer(x, w_gu, w_down, g, b, *, alpha, tm):
    n = x.shape[0]
    resident = functools.partial(pl.BlockSpec, index_map=lambda i: (0, 0), pipeline_mode=pl.Buffered(1))
    return pl.pallas_call(
        functools.partial(_ffn_kernel, alpha=alpha),
        grid=(n // tm,),
        in_specs=[
            pl.BlockSpec((tm, D_MODEL), lambda i: (i, 0)),
            resident((D_MODEL, 2 * D_FF)),
            resident((D_FF, D_MODEL)),
            resident((1, D_MODEL)),
            resident((1, D_MODEL)),
        ],
        out_specs=pl.BlockSpec((tm, D_MODEL), lambda i: (i, 0)),
        out_shape=jax.ShapeDtypeStruct((n, D_MODEL), F32),
        compiler_params=_params("parallel"),
        name="ffn_sublayer",
    )(x, w_gu, w_down, g, b)


def _proj_kernel(x_ref, *refs):
    w_refs = refs[:N_GROUPS]
    scale_ref = refs[N_GROUPS]
    out_refs = refs[N_GROUPS + 1:2 * N_GROUPS + 1]
    k32_ref, v32_ref, xb_ref = refs[2 * N_GROUPS + 1:]

    @pl.when(pl.program_id(1) == 0)
    def _():
        xb_ref[...] = x_ref[...].astype(BF16)

    xb = xb_ref[...]
    for grp in range(N_GROUPS):
        y = jnp.dot(xb, w_refs[grp][...], preferred_element_type=F32)
        if grp == G_QKR:
            y = y * scale_ref[...]
        out_refs[grp][...] = y.astype(out_refs[grp].dtype)
        if grp == G_KA:
            k32_ref[...] = y
        if grp == G_VA:
            v32_ref[...] = y


def _input_projection(x, w_in, qkr_scale, *, act_dtype, tm, tn):
    n = x.shape[0]
    nj = D_MODEL // tn
    w_specs = [pl.BlockSpec((D_MODEL, tn), functools.partial(lambda i, j, grp: (0, grp * nj + j), grp=grp))
               for grp in range(N_GROUPS)]
    out_spec = pl.BlockSpec((tm, tn), lambda i, j: (i, j))
    return pl.pallas_call(
        _proj_kernel,
        grid=(n // tm, nj),
        in_specs=[pl.BlockSpec((tm, D_MODEL), lambda i, j: (i, 0))] + w_specs
                 + [pl.BlockSpec((1, tn), lambda i, j: (0, j))],
        out_specs=[out_spec] * (N_GROUPS + 2),
        out_shape=[jax.ShapeDtypeStruct((n, D_MODEL), act_dtype)] * N_GROUPS
                  + [jax.ShapeDtypeStruct((n, D_MODEL), F32)] * 2,
        scratch_shapes=[pltpu.VMEM((tm, D_MODEL), BF16)],
        compiler_params=_params("parallel", "arbitrary"),
        name="input_projection",
    )(x, *([w_in] * N_GROUPS), qkr_scale)


def _attn_kernel(slopes_ref, q_ref, k_ref, vt_ref, lp_ref, g_ref, o_ref,
                 qst_ref, bias_ref, s_ref, p_ref, alpha_ref, m_ref, acc_ref, *, tq, lam_init):
    h = pl.program_id(1)
    i = pl.program_id(2)
    slope = slopes_ref[h]

    qt = (q_ref[...].astype(F32) * (DA_HEAD_DIM ** -0.5)).T.astype(BF16)
    sub = lax.broadcasted_iota(jnp.int32, qt.shape, 0)
    zero = jnp.zeros_like(qt)
    qst_ref[:, 0:tq] = jnp.where(sub < DA_HEAD_DIM, qt, zero)
    qst_ref[:, tq:2 * tq] = jnp.where(sub >= DA_HEAD_DIM, qt, zero)

    @pl.when(i == 0)
    def _():
        key = lax.broadcasted_iota(jnp.int32, (tq, tq), 0)
        qry = lax.broadcasted_iota(jnp.int32, (tq, tq), 1)
        rel = (key - qry).astype(F32) * slope
        diag = jnp.where(key <= qry, rel, NEG_BIG)
        bias_ref[0, :, 0:tq] = rel
        bias_ref[0, :, tq:2 * tq] = rel
        bias_ref[1, :, 0:tq] = diag
        bias_ref[1, :, tq:2 * tq] = diag

    m_ref[...] = jnp.full_like(m_ref, NEG_BIG)
    acc_ref[...] = jnp.zeros_like(acc_ref)

    def qk(j, slot):
        k = k_ref[pl.ds(j * tq, tq), :]
        s = jnp.dot(k, qst_ref[...], preferred_element_type=F32)
        s_ref[slot] = s + bias_ref[(j == i).astype(jnp.int32)]

    def softmax(j, slot):
        s = s_ref[slot]
        c = slope * ((j - i) * tq).astype(F32)
        m_prev = m_ref[...]
        m_new = jnp.maximum(m_prev, jnp.max(s, axis=0, keepdims=True) + c)
        alpha_ref[slot] = jnp.exp(m_prev - m_new)
        p_ref[slot] = jnp.exp(s - (m_new - c)).astype(BF16)
        m_ref[...] = m_new

    def pv(j, slot):
        acc_ref[...] = alpha_ref[slot] * acc_ref[...] + jnp.dot(
            vt_ref[j], p_ref[slot], preferred_element_type=F32)

    def tick(u, parity, *, with_pv=True, with_qk=True):
        if with_pv:
            pv(u - 2, parity)
        softmax(u - 1, 1 - parity)
        if with_qk:
            qk(u, parity)

    def two_ticks(t, carry):
        u = 2 * t
        softmax(u - 1, 1)
        qk(u, 0)
        qk(u + 1, 1)
        pv(u - 2, 0)
        softmax(u, 0)
        pv(u - 1, 1)
        return carry

    n = i + 1
    qk(0, 0)

    @pl.when(n >= 2)
    def _():
        tick(1, 1, with_pv=False)

    lax.fori_loop(1, jnp.maximum(n - 2, 0) // 2 + 1, two_ticks, 0)

    @pl.when((n >= 2) & (n % 2 == 0))
    def _():
        tick(n, 0, with_qk=False)
        pv(n - 1, 1)

    @pl.when((n >= 3) & (n % 2 == 1))
    def _():
        tick(n - 1, 0)
        tick(n, 1, with_qk=False)
        pv(n - 1, 0)

    @pl.when(n == 1)
    def _():
        softmax(0, 0)
        pv(0, 0)

    o = acc_ref[0:DA_V_DIM, :] / acc_ref[DA_V_DIM:DA_V_DIM + 1, :]
    lam = _diff_lambda(lp_ref, lam_init)
    d = o[:, 0:tq] - lam * o[:, tq:2 * tq]
    r = d * lax.rsqrt(jnp.mean(d * d, axis=0, keepdims=True) + LN_EPS) * g_ref[...] * (1.0 - lam_init)
    o_ref[...] = r.T.astype(o_ref.dtype)


def _prompt_attention(q, k, vt, lp, subln_g_col, *, lam_init, tq):
    bsz, seq, _ = q.shape
    assert tq & (tq - 1) == 0 and seq % tq == 0
    nk = seq // tq
    vt_rows = DA_V_DIM + ONES_ROWS
    return pl.pallas_call(
        functools.partial(_attn_kernel, tq=tq, lam_init=lam_init),
        grid=(bsz, DA_HEADS, nk),
        in_specs=[
            pl.BlockSpec(memory_space=pltpu.SMEM),
            pl.BlockSpec((None, tq, DA_V_DIM), lambda b, h, i: (b, i, h)),
            pl.BlockSpec((None, seq, DA_V_DIM), lambda b, h, i: (b, 0, h)),
            pl.BlockSpec((None, None, nk, vt_rows, tq), lambda b, h, i: (b, h, 0, 0, 0)),
            pl.BlockSpec((4, DA_HEAD_DIM), lambda b, h, i: (0, 0)),
            pl.BlockSpec((DA_V_DIM, 1), lambda b, h, i: (0, 0)),
        ],
        out_specs=pl.BlockSpec((None, tq, DA_V_DIM), lambda b, h, i: (b, i, h)),
        out_shape=jax.ShapeDtypeStruct((bsz, seq, DA_HEADS * DA_V_DIM), BF16),
        scratch_shapes=[
            pltpu.VMEM((DA_V_DIM, 2 * tq), BF16),
            pltpu.VMEM((2, tq, 2 * tq), F32),
            pltpu.VMEM((2, tq, 2 * tq), F32),
            pltpu.VMEM((2, tq, 2 * tq), BF16),
            pltpu.VMEM((2, 1, 2 * tq), F32),
            pltpu.VMEM((1, 2 * tq), F32),
            pltpu.VMEM((vt_rows, 2 * tq), F32),
        ],
        compiler_params=_params("parallel", "parallel", "arbitrary"),
        name="prompt_diff_attention",
    )(jnp.asarray(_alibi_slopes()), q, k, vt, lp, subln_g_col)


def _ret_kernel(qk_ref, v_ref, gr_ref, o_ref, sfin_ref, state_ref, decay_ref, cross_ref, kdec_ref, *, chunk):
    c = pl.program_id(1)
    log_g = _retention_log_decay()

    @pl.when(c == 0)
    def _():
        state_ref[...] = jnp.zeros_like(state_ref)
        row = lax.broadcasted_iota(jnp.int32, (chunk, chunk), 0)
        col = lax.broadcasted_iota(jnp.int32, (chunk, chunk), 1)
        diff = (row - col).astype(F32)
        idx_v = lax.broadcasted_iota(jnp.int32, (chunk, RET_V_DIM), 0).astype(F32)
        idx_k = lax.broadcasted_iota(jnp.int32, (chunk, RET_K_DIM), 0).astype(F32)
        for h in range(RET_HEADS):
            lg = float(log_g[h])
            decay_ref[h] = jnp.where(diff >= 0, jnp.exp(jnp.maximum(diff, 0.0) * lg), 0.0)
            cross_ref[h] = jnp.exp((idx_v + 1.0) * lg)
            kdec_ref[h] = jnp.exp((chunk - 1.0 - idx_k) * lg)

    for h in range(RET_HEADS):
        q = qk_ref[:, h * RET_K_DIM:(h + 1) * RET_K_DIM]
        k = qk_ref[:, (RET_HEADS + h) * RET_K_DIM:(RET_HEADS + h + 1) * RET_K_DIM]
        v = v_ref[:, h * RET_V_DIM:(h + 1) * RET_V_DIM]
        gr = gr_ref[:, h * RET_V_DIM:(h + 1) * RET_V_DIM]
        scores = lax.dot_general(q, k, _NT, preferred_element_type=F32) * decay_ref[h]
        state = state_ref[h]
        o = (jnp.dot(scores.astype(BF16), v, preferred_element_type=F32)
             + jnp.dot(q, state.astype(BF16), preferred_element_type=F32) * cross_ref[h])
        k_w = (k.astype(F32) * kdec_ref[h]).astype(BF16)
        new_state = math.exp(chunk * float(log_g[h])) * state + lax.dot_general(
            k_w, v, (((0,), (0,)), ((), ())), preferred_element_type=F32)
        state_ref[h] = new_state
        o_ref[:, h * RET_V_DIM:(h + 1) * RET_V_DIM] = (_rms_norm(o) * _silu(gr.astype(F32))).astype(o_ref.dtype)

    @pl.when(c == pl.num_programs(1) - 1)
    def _():
        sfin_ref[...] = state_ref[...]


def _prompt_retention(qk, v, gr, *, chunk):
    bsz, seq, _ = qk.shape
    row_spec = pl.BlockSpec((None, chunk, D_MODEL), lambda b, c: (b, c, 0))
    state_shape = (RET_HEADS, RET_K_DIM, RET_V_DIM)
    return pl.pallas_call(
        functools.partial(_ret_kernel, chunk=chunk),
        grid=(bsz, seq // chunk),
        in_specs=[row_spec, row_spec, row_spec],
        out_specs=[row_spec, pl.BlockSpec((None,) + state_shape, lambda b, c: (b, 0, 0, 0))],
        out_shape=[
            jax.ShapeDtypeStruct((bsz, seq, RET_HEADS * RET_V_DIM), BF16),
            jax.ShapeDtypeStruct((bsz,) + state_shape, F32),
        ],
        scratch_shapes=[
            pltpu.VMEM(state_shape, F32),
            pltpu.VMEM((RET_HEADS, chunk, chunk), F32),
            pltpu.VMEM((RET_HEADS, chunk, RET_V_DIM), F32),
            pltpu.VMEM((RET_HEADS, chunk, RET_K_DIM), F32),
        ],
        compiler_params=_params("parallel", "arbitrary"),
        name="prompt_retention",
    )(qk, v, gr)


def _merge_kernel(x_ref, oa_ref, ob_ref, ga_ref, gb_ref, wa_ref, wb_ref, wo_ref, g_ref, b_ref, o_ref, *, alpha):
    a = jnp.dot(oa_ref[...].astype(BF16), wa_ref[...], preferred_element_type=F32)
    b = jnp.dot(ob_ref[...].astype(BF16), wb_ref[...], preferred_element_type=F32)
    m = jax.nn.sigmoid(ga_ref[...].astype(F32)) * a + jax.nn.sigmoid(gb_ref[...].astype(F32)) * b
    y = alpha * x_ref[...] + jnp.dot(m.astype(BF16), wo_ref[...], preferred_element_type=F32)
    o_ref[...] = _layer_norm(y, g_ref[...], b_ref[...])


def _merge_sublayer(x, oa, ob, ga, gb, w_a, w_b, w_o, g, b, *, alpha, tm):
    n = x.shape[0]
    row_spec = pl.BlockSpec((tm, D_MODEL), lambda i: (i, 0))
    w_spec = pl.BlockSpec((D_MODEL, D_MODEL), lambda i: (0, 0))
    vec_spec = pl.BlockSpec((1, D_MODEL), lambda i: (0, 0))
    return pl.pallas_call(
        functools.partial(_merge_kernel, alpha=alpha),
        grid=(n // tm,),
        in_specs=[row_spec] * 5 + [w_spec] * 3 + [vec_spec] * 2,
        out_specs=row_spec,
        out_shape=jax.ShapeDtypeStruct((n, D_MODEL), F32),
        compiler_params=_params("parallel"),
        name="merge_sublayer",
    )(x, oa, ob, ga, gb, w_a, w_b, w_o, g, b)


def _dec_attn_kernel(pt_ref, q_ref, slope_ref, *refs, t_dec, lam_init, n_pp):
    del pt_ref
    k_refs, v_refs = refs[:n_pp], refs[n_pp:2 * n_pp]
    kn_ref, vn_ref, lp_ref, g_ref, o_ref, mb_ref, mbn_ref, m_ref, l_ref, acc_ref = refs[2 * n_pp:]
    p = pl.program_id(1)
    n_rows = 2 * t_dec * DA_HEADS
    page_rows = PAGE_SIZE * DA_HEADS

    @pl.when(p == 0)
    def _():
        m_ref[...] = jnp.full_like(m_ref, NEG_BIG)
        l_ref[...] = jnp.zeros_like(l_ref)
        acc_ref[...] = jnp.zeros_like(acc_ref)
        row = lax.broadcasted_iota(jnp.int32, (n_rows, page_rows), 0)
        col = lax.broadcasted_iota(jnp.int32, (n_rows, page_rows), 1)
        same_head = ((col - row) & (DA_HEADS - 1)) == 0
        tok = (row // DA_HEADS) % t_dec
        rel = (col // DA_HEADS - tok - PAST_LEN).astype(F32)
        mb_ref[...] = jnp.where(same_head, slope_ref[...] * rel, NEG_BIG)
        rown = lax.broadcasted_iota(jnp.int32, (n_rows, PAGE_SIZE), 0)
        coln = lax.broadcasted_iota(jnp.int32, (n_rows, PAGE_SIZE), 1)
        tokq = (rown // DA_HEADS) % t_dec
        tokk = coln // DA_HEADS
        ok = (((coln - rown) & (DA_HEADS - 1)) == 0) & (tokk <= tokq)
        mbn_ref[...] = jnp.where(ok, slope_ref[...] * (tokk - tokq).astype(F32), NEG_BIG)

    def update(blocks):
        q = q_ref[...]
        scores = [lax.dot_general(q, kf.astype(BF16), _NT, preferred_element_type=F32) + mb
                  for kf, _, mb in blocks]
        m_prev = m_ref[...]
        m_new = m_prev
        for s in scores:
            m_new = jnp.maximum(m_new, jnp.max(s, axis=-1, keepdims=True))
        alpha = jnp.exp(m_prev - m_new)
        l_new = alpha * l_ref[...]
        acc_new = alpha * acc_ref[...]
        for s, (_, vf, _) in zip(scores, blocks):
            pr = jnp.exp(s - m_new)
            l_new = l_new + jnp.sum(pr, axis=-1, keepdims=True)
            acc_new = acc_new + jnp.dot(pr.astype(BF16), vf.astype(BF16), preferred_element_type=F32)
        l_ref[...] = l_new
        acc_ref[...] = acc_new
        m_ref[...] = m_new

    slope = slope_ref[...]
    update([(k_refs[r][...].reshape(page_rows, DA_V_DIM), v_refs[r][...].reshape(page_rows, DA_V_DIM),
             mb_ref[...] + slope * ((p * n_pp + r) * PAGE_SIZE).astype(F32)) for r in range(n_pp)])

    @pl.when(p == pl.num_programs(1) - 1)
    def _():
        update([(kn_ref[...], vn_ref[...], mbn_ref[...])])
        o = acc_ref[...] / l_ref[...]
        lam = _diff_lambda(lp_ref, lam_init)
        half = n_rows // 2
        d = o[0:half] - lam * o[half:n_rows]
        o_ref[...] = _rms_norm(d) * g_ref[...] * (1.0 - lam_init)


def _sample_attention(page_table, q_rows, cache_k, cache_v, k_new, v_new, lp, subln_g, *,
                      t_dec, lam_init, n_pp):
    n_dec, n_pages = page_table.shape
    n_rows = 2 * t_dec * DA_HEADS
    assert t_dec * DA_HEADS <= PAGE_SIZE and n_pages % n_pp == 0
    slope_rows = jnp.asarray(np.tile(_alibi_slopes(), 2 * t_dec).reshape(n_rows, 1))
    page_specs = [pl.BlockSpec((None, PAGE_SIZE, DA_HEADS, DA_V_DIM),
                               functools.partial(lambda s, p, pt, r: (pt[s, p * n_pp + r], 0, 0, 0), r=r))
                  for r in range(n_pp)]
    new_spec = pl.BlockSpec((None, PAGE_SIZE, DA_V_DIM), lambda s, p, pt: (s, 0, 0))
    return pl.pallas_call(
        functools.partial(_dec_attn_kernel, t_dec=t_dec, lam_init=lam_init, n_pp=n_pp),
        grid_spec=pltpu.PrefetchScalarGridSpec(
            num_scalar_prefetch=1,
            grid=(n_dec, n_pages // n_pp),
            in_specs=[
                pl.BlockSpec((None, n_rows, DA_V_DIM), lambda s, p, pt: (s, 0, 0)),
                pl.BlockSpec((n_rows, 1), lambda s, p, pt: (0, 0)),
                *page_specs, *page_specs, new_spec, new_spec,
                pl.BlockSpec((4, DA_HEAD_DIM), lambda s, p, pt: (0, 0)),
                pl.BlockSpec((1, DA_V_DIM), lambda s, p, pt: (0, 0)),
            ],
            out_specs=pl.BlockSpec((None, n_rows // 2, DA_V_DIM), lambda s, p, pt: (s, 0, 0)),
            scratch_shapes=[
                pltpu.VMEM((n_rows, PAGE_SIZE * DA_HEADS), F32),
                pltpu.VMEM((n_rows, PAGE_SIZE), F32),
                pltpu.VMEM((n_rows, 1), F32),
                pltpu.VMEM((n_rows, 1), F32),
                pltpu.VMEM((n_rows, DA_V_DIM), F32),
            ],
        ),
        out_shape=jax.ShapeDtypeStruct((n_dec, n_rows // 2, DA_V_DIM), F32),
        compiler_params=_params("parallel", "arbitrary"),
        name="sample_diff_attention",
    )(page_table, q_rows, slope_rows, *([cache_k] * n_pp), *([cache_v] * n_pp), k_new, v_new, lp, subln_g)


def _dec_ret_kernel(qk_ref, kt_ref, v_ref, gr_ref, st_ref, o_ref, snew_ref, *, t_dec, t_pad):
    log_g = _retention_log_decay()
    qk_all = qk_ref[...]
    v_all = v_ref[...]
    gr_all = gr_ref[...]
    tok = lax.broadcasted_iota(jnp.int32, (t_pad, 1), 0).astype(F32)
    for h in range(RET_HEADS):
        lg = float(log_g[h])
        q = qk_all[:, h * RET_K_DIM:(h + 1) * RET_K_DIM]
        k = qk_all[:, (RET_HEADS + h) * RET_K_DIM:(RET_HEADS + h + 1) * RET_K_DIM]
        v = v_all[:, h * RET_V_DIM:(h + 1) * RET_V_DIM]
        gr = gr_all[:, h * RET_V_DIM:(h + 1) * RET_V_DIM]
        kt = kt_ref[h]
        state = st_ref[h]
        o = jnp.dot(q.astype(BF16), state.astype(BF16), preferred_element_type=F32) * jnp.exp((tok + 1.0) * lg)
        new_state = math.exp(t_dec * lg) * state
        for j in range(t_dec):
            kj = kt[:, j:j + 1]
            vj = v[j:j + 1, :]
            s_j = jnp.sum(q * k[j:j + 1, :], axis=-1, keepdims=True)
            d_j = jnp.where(tok >= j, jnp.exp(jnp.maximum(tok - j, 0.0) * lg), 0.0)
            o = o + (s_j * d_j) * vj
            new_state = new_state + (kj * math.exp((t_dec - 1.0 - j) * lg)) * vj
        snew_ref[h] = new_state
        o_ref[:, h * RET_V_DIM:(h + 1) * RET_V_DIM] = _rms_norm(o) * _silu(gr)


def _sample_retention(qk, kt, v, gr, state, *, t_dec):
    n_dec, t_pad, _ = qk.shape
    return pl.pallas_call(
        functools.partial(_dec_ret_kernel, t_dec=t_dec, t_pad=t_pad),
        grid=(n_dec,),
        in_specs=[
            pl.BlockSpec((None, t_pad, 2 * RET_HEADS * RET_K_DIM), lambda s: (s, 0, 0)),
            pl.BlockSpec((None, RET_HEADS, RET_K_DIM, t_pad), lambda s: (s, 0, 0, 0)),
            pl.BlockSpec((None, t_pad, RET_HEADS * RET_V_DIM), lambda s: (s, 0, 0)),
            pl.BlockSpec((None, t_pad, RET_HEADS * RET_V_DIM), lambda s: (s, 0, 0)),
            pl.BlockSpec((None, RET_HEADS, RET_K_DIM, RET_V_DIM), lambda s: (s, 0, 0, 0)),
        ],
        out_specs=[
            pl.BlockSpec((None, t_pad, RET_HEADS * RET_V_DIM), lambda s: (s, 0, 0)),
            pl.BlockSpec((None, RET_HEADS, RET_K_DIM, RET_V_DIM), lambda s: (s, 0, 0, 0)),
        ],
        out_shape=[
            jax.ShapeDtypeStruct((n_dec, t_pad, RET_HEADS * RET_V_DIM), F32),
            jax.ShapeDtypeStruct(state.shape, F32),
        ],
        compiler_params=_params("parallel"),
        name="sample_retention",
    )(qk, kt, v, gr, state)


def _sample_attention_inputs(q, k, v, n_dec, t_dec):
    qh = q.reshape(n_dec, t_dec, DA_HEADS, DA_V_DIM) * (DA_HEAD_DIM ** -0.5)
    first = jnp.arange(DA_V_DIM) < DA_HEAD_DIM
    q_rows = jnp.stack([jnp.where(first, qh, 0.0), jnp.where(first, 0.0, qh)], axis=1)
    q_rows = q_rows.reshape(n_dec, 2 * t_dec * DA_HEADS, DA_V_DIM).astype(BF16)
    pad = ((0, 0), (0, PAGE_SIZE - t_dec * DA_HEADS), (0, 0))
    k_new = jnp.pad(k.reshape(n_dec, t_dec * DA_HEADS, DA_V_DIM), pad)
    v_new = jnp.pad(v.reshape(n_dec, t_dec * DA_HEADS, DA_V_DIM), pad)
    return q_rows, k_new, v_new


def kernel(x_prompt, x_sample, cache_k, cache_v, state_ret, page_table, ln_g, ln_b, ffn1_w_gu, ffn1_w_down,
           w_in, da_lambda, da_subln_g, w_branch_a, w_branch_b, w_out, ffn2_w_gu, ffn2_w_down):
    depth = w_in.shape[0]
    bsz, seq, _ = x_prompt.shape
    n_dec, t_dec, _ = x_sample.shape
    alpha = (2.0 * depth) ** 0.25
    t_pad = 8
    tm_p, tm_s = 512, n_dec * t_dec
    tq = 256

    xp = x_prompt.reshape(bsz * seq, D_MODEL)
    xs = x_sample.reshape(n_dec * t_dec, D_MODEL)
    qkr_scale = jnp.concatenate([jnp.ones((RET_HEADS * RET_K_DIM,), F32),
                                 jnp.full((RET_HEADS * RET_K_DIM,), RET_K_DIM ** -0.5, F32)]).reshape(1, D_MODEL)
    outs = [[] for _ in range(6)]
    for l in range(depth):
        lam_init = _lambda_init(l)
        w1_gu, w1_d = ffn1_w_gu[l].astype(BF16), ffn1_w_down[l].astype(BF16)
        w2_gu, w2_d = ffn2_w_gu[l].astype(BF16), ffn2_w_down[l].astype(BF16)
        w_in_l = w_in[l].astype(BF16)
        w_a, w_b, w_o = (w[l].astype(BF16) for w in (w_branch_a, w_branch_b, w_out))
        ln = [(ln_g[l, s].reshape(1, D_MODEL), ln_b[l, s].reshape(1, D_MODEL)) for s in range(3)]
        lp = da_lambda[l].astype(F32)
        subln = da_subln_g[l].reshape(1, DA_V_DIM)

        xp = _ffn_sublayer(xp, w1_gu, w1_d, *ln[0], alpha=alpha, tm=tm_p)
        proj = _input_projection(xp, w_in_l, qkr_scale, act_dtype=BF16, tm=tm_p, tn=512)
        k32, v32 = proj[N_GROUPS], proj[N_GROUPS + 1]
        as_seq = lambda t: t.reshape(bsz, seq, D_MODEL)
        vt = proj[G_VA].reshape(bsz, seq // tq, tq, DA_HEADS, DA_V_DIM).transpose(0, 3, 1, 4, 2)
        vt = jnp.concatenate([vt, jnp.ones((bsz, DA_HEADS, seq // tq, ONES_ROWS, tq), BF16)], axis=3)
        oa = _prompt_attention(as_seq(proj[G_QA]), as_seq(proj[G_KA]), vt, lp, subln.reshape(DA_V_DIM, 1),
                               lam_init=lam_init, tq=tq)
        ob, s_fin = _prompt_retention(as_seq(proj[G_QKR]), as_seq(proj[G_VR]), as_seq(proj[G_GR]), chunk=128)
        xp = _merge_sublayer(xp, oa.reshape(bsz * seq, D_MODEL), ob.reshape(bsz * seq, D_MODEL),
                             proj[G_GA], proj[G_GB], w_a, w_b, w_o, *ln[1], alpha=alpha, tm=tm_p)
        xp = _ffn_sublayer(xp, w2_gu, w2_d, *ln[2], alpha=alpha, tm=tm_p)
        outs[0].append(k32.reshape(bsz, seq, DA_HEADS, DA_V_DIM))
        outs[1].append(v32.reshape(bsz, seq, DA_HEADS, DA_V_DIM))
        outs[2].append(s_fin)

        xs = _ffn_sublayer(xs, w1_gu, w1_d, *ln[0], alpha=alpha, tm=tm_s)
        proj = _input_projection(xs, w_in_l, qkr_scale, act_dtype=F32, tm=tm_s, tn=512)
        k32, v32 = proj[N_GROUPS], proj[N_GROUPS + 1]
        q_rows, k_new, v_new = _sample_attention_inputs(proj[G_QA], k32, v32, n_dec, t_dec)
        oa = _sample_attention(page_table, q_rows, cache_k[l], cache_v[l], k_new, v_new, lp, subln,
                               t_dec=t_dec, lam_init=lam_init, n_pp=8)
        tok_pad = lambda t: jnp.pad(t.reshape(n_dec, t_dec, -1), ((0, 0), (0, t_pad - t_dec), (0, 0)))
        qk_r = tok_pad(proj[G_QKR])
        kt_r = qk_r[:, :, RET_HEADS * RET_K_DIM:].reshape(n_dec, t_pad, RET_HEADS, RET_K_DIM)
        kt_r = kt_r.transpose(0, 2, 3, 1)
        ob, s_new = _sample_retention(qk_r, kt_r, tok_pad(proj[G_VR]), tok_pad(proj[G_GR]), state_ret[l],
                                      t_dec=t_dec)
        ob = ob[:, :t_dec].reshape(n_dec * t_dec, D_MODEL)
        xs = _merge_sublayer(xs, oa.reshape(n_dec * t_dec, D_MODEL), ob, proj[G_GA], proj[G_GB],
                             w_a, w_b, w_o, *ln[1], alpha=alpha, tm=tm_s)
        xs = _ffn_sublayer(xs, w2_gu, w2_d, *ln[2], alpha=alpha, tm=tm_s)
        outs[3].append(k32.reshape(n_dec, t_dec, DA_HEADS, DA_V_DIM))
        outs[4].append(v32.reshape(n_dec, t_dec, DA_HEADS, DA_V_DIM))
        outs[5].append(s_new)

    k_p, v_p, s_p, k_s, v_s, s_s = (jnp.stack(o) for o in outs)
    return (xp.reshape(bsz, seq, D_MODEL), xs.reshape(n_dec, t_dec, D_MODEL), k_p, v_p, s_p, k_s, v_s, s_s)
```

```python
import functools
import math

import numpy as np
import jax
import jax.numpy as jnp
from jax import lax
from jax.experimental import pallas as pl
from jax.experimental.pallas import tpu as pltpu

D_MODEL = 1024
PAST_LEN = 8192
PAGE_SIZE = 128
DA_HEADS = 8
DA_HEAD_DIM = 64
DA_V_DIM = 2 * DA_HEAD_DIM
RET_HEADS = 4
RET_K_DIM = 128
RET_V_DIM = 256
D_FF = 2816
LN_EPS = 1e-5
N_GROUPS = 8
G_QA, G_KA, G_VA, G_QKR, G_VR, G_GR, G_GA, G_GB = range(N_GROUPS)

VMEM_LIMIT_BYTES = 48 * 1024 * 1024
NEG_BIG = -1e30
ONES_ROWS = 16
MXU_DIM = 256
FF_CHUNKS = tuple((s, min(3 * MXU_DIM, D_FF - s)) for s in range(0, D_FF, 3 * MXU_DIM))
assert all(size % MXU_DIM == 0 for _, size in FF_CHUNKS)

F32 = jnp.float32
BF16 = jnp.bfloat16
_NT = (((1,), (1,)), ((), ()))


def _alibi_slopes():
    return (2.0 ** (-8.0 * np.arange(1, DA_HEADS + 1) / DA_HEADS)).astype(np.float32)


def _retention_log_decay():
    return np.log(1.0 - 2.0 ** (-5.0 - np.arange(RET_HEADS))).astype(np.float32)


def _lambda_init(layer):
    return 0.8 - 0.6 * math.exp(-0.3 * layer)


def _params(*semantics):
    return pltpu.CompilerParams(dimension_semantics=semantics, vmem_limit_bytes=VMEM_LIMIT_BYTES)


def _layer_norm(y, g, b):
    mu = jnp.mean(y, axis=-1, keepdims=True)
    d = y - mu
    var = jnp.mean(d * d, axis=-1, keepdims=True)
    return d * lax.rsqrt(var + LN_EPS) * g + b


def _rms_norm(o):
    return o * lax.rsqrt(jnp.mean(o * o, axis=-1, keepdims=True) + LN_EPS)


def _silu(a):
    return a * jax.nn.sigmoid(a)


def _diff_lambda(lp_ref, lam_init):
    lp = lp_ref[...]
    a = jnp.sum(lp[0:1] * lp[1:2], axis=-1, keepdims=True)
    b = jnp.sum(lp[2:3] * lp[3:4], axis=-1, keepdims=True)
    return jnp.exp(a) - jnp.exp(b) + lam_init


def _ffn_kernel(x_ref, wgu_ref, wd_ref, g_ref, b_ref, o_ref, *, alpha):
    x = x_ref[...]
    xb = x.astype(BF16)
    acc = None
    for start, size in FF_CHUNKS:
        gate = jnp.dot(xb, wgu_ref[:, start:start + size], preferred_element_type=F32)
        up = jnp.dot(xb, wgu_ref[:, D_FF + start:D_FF + start + size], preferred_element_type=F32)
        h = (_silu(gate) * up).astype(BF16)
        part = jnp.dot(h, wd_ref[start:start + size, :], preferred_element_type=F32)
        acc = part if acc is None else acc + part
    o_ref[...] = _layer_norm(alpha * x + 0.5 * acc, g_ref[...], b_ref[...])


def _ffn_sublayer(x, w_gu, w_down, g, b, *, alpha, tm):
    n = x.shape[0]
    resident = functools.partial(pl.BlockSpec, index_map=lambda i: (0, 0), pipeline_mode=pl.Buffered(1))
    return pl.pallas_call(
        functools.partial(_ffn_kernel, alpha=alpha),
        grid=(n // tm,),
        in_specs=[
            pl.BlockSpec((tm, D_MODEL), lambda i: (i, 0)),
            resident((D_MODEL, 2 * D_FF)),
            resident((D_FF, D_MODEL)),
            resident((1, D_MODEL)),
            resident((1, D_MODEL)),
        ],
        out_specs=pl.BlockSpec((tm, D_MODEL), lambda i: (i, 0)),
        out_shape=jax.ShapeDtypeStruct((n, D_MODEL), F32),
        compiler_params=_params("parallel"),
        name="ffn_sublayer",
    )(x, w_gu, w_down, g, b)


def _proj_kernel(x_ref, *refs):
    w_refs = refs[:N_GROUPS]
    scale_ref = refs[N_GROUPS]
    out_refs = refs[N_GROUPS + 1:2 * N_GROUPS + 1]
    k32_ref, v32_ref, xb_ref = refs[2 * N_GROUPS + 1:]

    @pl.when(pl.program_id(1) == 0)
    def _():
        xb_ref[...] = x_ref[...].astype(BF16)

    xb = xb_ref[...]
    for grp in range(N_GROUPS):
        y = jnp.dot(xb, w_refs[grp][...], preferred_element_type=F32)
        if grp == G_QKR:
            y = y * scale_ref[...]
        out_refs[grp][...] = y.astype(out_refs[grp].dtype)
        if grp == G_KA:
            k32_ref[...] = y
        if grp == G_VA:
            v32_ref[...] = y


def _input_projection(x, w_in, qkr_scale, *, act_dtype, tm, tn):
    n = x.shape[0]
    nj = D_MODEL // tn
    w_specs = [pl.BlockSpec((D_MODEL, tn), functools.partial(lambda i, j, grp: (0, grp * nj + j), grp=grp))
               for grp in range(N_GROUPS)]
    out_spec = pl.BlockSpec((tm, tn), lambda i, j: (i, j))
    return pl.pallas_call(
        _proj_kernel,
        grid=(n // tm, nj),
        in_specs=[pl.BlockSpec((tm, D_MODEL), lambda i, j: (i, 0))] + w_specs
                 + [pl.BlockSpec((1, tn), lambda i, j: (0, j))],
        out_specs=[out_spec] * (N_GROUPS + 2),
        out_shape=[jax.ShapeDtypeStruct((n, D_MODEL), act_dtype)] * N_GROUPS
                  + [jax.ShapeDtypeStruct((n, D_MODEL), F32)] * 2,
        scratch_shapes=[pltpu.VMEM((tm, D_MODEL), BF16)],
        compiler_params=_params("parallel", "arbitrary"),
        name="input_projection",
    )(x, *([w_in] * N_GROUPS), qkr_scale)


HEADS_PER_STEP = 2


def _attn_kernel(slopes_ref, q_ref, k_ref, vt_ref, lp_ref, g_ref, o_ref,
                 qst_ref, kfeat_ref, mask_ref, s_ref, p_ref, alpha_ref, m_ref, acc_ref, *, tq, lam_init):
    hp = pl.program_id(1)
    i = pl.program_id(2)
    heads = range(HEADS_PER_STEP)
    slopes = [slopes_ref[hp * HEADS_PER_STEP + hh] for hh in heads]

    @pl.when(i == 0)
    def _():
        sub = lax.broadcasted_iota(jnp.int32, (DA_V_DIM, 2 * tq), 0)
        qrel = (lax.broadcasted_iota(jnp.int32, (DA_V_DIM, 2 * tq), 1) & (tq - 1)).astype(F32)
        for hh in heads:
            feat = jnp.where(sub == 0, slopes[hh], jnp.where(sub == 1, -slopes[hh] * qrel, 0.0))
            qst_ref[hh, DA_V_DIM:2 * DA_V_DIM, :] = feat.astype(BF16)
        lane = lax.broadcasted_iota(jnp.int32, (tq, DA_V_DIM), 1)
        krel = lax.broadcasted_iota(jnp.int32, (tq, DA_V_DIM), 0).astype(F32)
        kfeat_ref[...] = jnp.where(lane == 0, krel, jnp.where(lane == 1, 1.0, 0.0)).astype(BF16)
        key = lax.broadcasted_iota(jnp.int32, (tq, 2 * tq), 0)
        qry = lax.broadcasted_iota(jnp.int32, (tq, 2 * tq), 1) & (tq - 1)
        mask_ref[...] = jnp.where(key <= qry, 0.0, NEG_BIG)

    for hh in heads:
        q = q_ref[:, hh * DA_V_DIM:(hh + 1) * DA_V_DIM]
        qt = (q.astype(F32) * (DA_HEAD_DIM ** -0.5)).T.astype(BF16)
        sub = lax.broadcasted_iota(jnp.int32, qt.shape, 0)
        zero = jnp.zeros_like(qt)
        qst_ref[hh, 0:DA_V_DIM, 0:tq] = jnp.where(sub < DA_HEAD_DIM, qt, zero)
        qst_ref[hh, 0:DA_V_DIM, tq:2 * tq] = jnp.where(sub >= DA_HEAD_DIM, qt, zero)

    m_ref[...] = jnp.full_like(m_ref, NEG_BIG)
    acc_ref[...] = jnp.zeros_like(acc_ref)

    def block_of(r):
        if isinstance(r, int):
            return i if r == 0 else r - 1
        return jnp.where(r == 0, i, r - 1)

    def qk(r, slot):
        j = block_of(r)
        for hh in heads:
            k = k_ref[pl.ds(j * tq, tq), hh * DA_V_DIM:(hh + 1) * DA_V_DIM]
            k_aug = jnp.concatenate([k, kfeat_ref[...]], axis=1)
            s = jnp.dot(k_aug, qst_ref[hh], preferred_element_type=F32)
            if isinstance(r, int) and r == 0:
                s = s + mask_ref[...]
            s_ref[slot, hh] = s

    def softmax(r, slot):
        j = block_of(r)
        for hh in heads:
            s = s_ref[slot, hh]
            c = slopes[hh] * ((j - i) * tq).astype(F32)
            m_prev = m_ref[hh]
            m_new = jnp.maximum(m_prev, jnp.max(s, axis=0, keepdims=True) + c)
            alpha_ref[slot, hh] = jnp.exp(m_prev - m_new)
            p_ref[slot, hh] = jnp.exp(s - (m_new - c)).astype(BF16)
            m_ref[hh] = m_new

    def pv(r, slot):
        j = block_of(r)
        for hh in heads:
            acc_ref[hh] = alpha_ref[slot, hh] * acc_ref[hh] + jnp.dot(
                vt_ref[hh, j], p_ref[slot, hh], preferred_element_type=F32)

    def tick(u, parity, *, with_pv=True, with_qk=True):
        if with_pv:
            pv(u - 2, parity)
        softmax(u - 1, 1 - parity)
        if with_qk:
            qk(u, parity)

    def two_ticks(t, carry):
        u = 2 * t
        softmax(u - 1, 1)
        qk(u, 0)
        qk(u + 1, 1)
        pv(u - 2, 0)
        softmax(u, 0)
        pv(u - 1, 1)
        return carry

    n = i + 1
    qk(0, 0)

    @pl.when(n >= 2)
    def _():
        tick(1, 1, with_pv=False)

    lax.fori_loop(1, jnp.maximum(n - 2, 0) // 2 + 1, two_ticks, 0)

    @pl.when((n >= 2) & (n % 2 == 0))
    def _():
        tick(n, 0, with_qk=False)
        pv(n - 1, 1)

    @pl.when((n >= 3) & (n % 2 == 1))
    def _():
        tick(n - 1, 0)
        tick(n, 1, with_qk=False)
        pv(n - 1, 0)

    @pl.when(n == 1)
    def _():
        softmax(0, 0)
        pv(0, 0)

    lam = _diff_lambda(lp_ref, lam_init)
    for hh in heads:
        o = acc_ref[hh, 0:DA_V_DIM, :] / acc_ref[hh, DA_V_DIM:DA_V_DIM + 1, :]
        d = o[:, 0:tq] - lam * o[:, tq:2 * tq]
        r = d * lax.rsqrt(jnp.mean(d * d, axis=0, keepdims=True) + LN_EPS) * g_ref[...] * (1.0 - lam_init)
        o_ref[:, hh * DA_V_DIM:(hh + 1) * DA_V_DIM] = r.T.astype(o_ref.dtype)


def _prompt_attention(q, k, vt, lp, subln_g_col, *, lam_init, tq):
    bsz, seq, _ = q.shape
    assert tq & (tq - 1) == 0 and tq <= 256 and seq % tq == 0 and DA_HEADS % HEADS_PER_STEP == 0
    nk = seq // tq
    vt_rows = DA_V_DIM + ONES_ROWS
    hw = HEADS_PER_STEP * DA_V_DIM
    return pl.pallas_call(
        functools.partial(_attn_kernel, tq=tq, lam_init=lam_init),
        grid=(bsz, DA_HEADS // HEADS_PER_STEP, nk),
        in_specs=[
            pl.BlockSpec(memory_space=pltpu.SMEM),
            pl.BlockSpec((None, tq, hw), lambda b, h, i: (b, i, h)),
            pl.BlockSpec((None, seq, hw), lambda b, h, i: (b, 0, h)),
            pl.BlockSpec((None, HEADS_PER_STEP, nk, vt_rows, tq), lambda b, h, i: (b, h, 0, 0, 0)),
            pl.BlockSpec((4, DA_HEAD_DIM), lambda b, h, i: (0, 0)),
            pl.BlockSpec((DA_V_DIM, 1), lambda b, h, i: (0, 0)),
        ],
        out_specs=pl.BlockSpec((None, tq, hw), lambda b, h, i: (b, i, h)),
        out_shape=jax.ShapeDtypeStruct((bsz, seq, DA_HEADS * DA_V_DIM), BF16),
        scratch_shapes=[
            pltpu.VMEM((HEADS_PER_STEP, 2 * DA_V_DIM, 2 * tq), BF16),
            pltpu.VMEM((tq, DA_V_DIM), BF16),
            pltpu.VMEM((tq, 2 * tq), F32),
            pltpu.VMEM((2, HEADS_PER_STEP, tq, 2 * tq), F32),
            pltpu.VMEM((2, HEADS_PER_STEP, tq, 2 * tq), BF16),
            pltpu.VMEM((2, HEADS_PER_STEP, 1, 2 * tq), F32),
            pltpu.VMEM((HEADS_PER_STEP, 1, 2 * tq), F32),
            pltpu.VMEM((HEADS_PER_STEP, vt_rows, 2 * tq), F32),
        ],
        compiler_params=_params("parallel", "parallel", "arbitrary"),
        name="prompt_diff_attention",
    )(jnp.asarray(_alibi_slopes()), q, k, vt, lp, subln_g_col)


def _ret_kernel(qk_ref, v_ref, gr_ref, o_ref, sfin_ref, state_ref, decay_ref, cross_ref, kdec_ref, *, chunk):
    c = pl.program_id(1)
    log_g = _retention_log_decay()

    @pl.when(c == 0)
    def _():
        state_ref[...] = jnp.zeros_like(state_ref)
        row = lax.broadcasted_iota(jnp.int32, (chunk, chunk), 0)
        col = lax.broadcasted_iota(jnp.int32, (chunk, chunk), 1)
        diff = (row - col).astype(F32)
        idx_v = lax.broadcasted_iota(jnp.int32, (chunk, RET_V_DIM), 0).astype(F32)
        idx_k = lax.broadcasted_iota(jnp.int32, (chunk, RET_K_DIM), 0).astype(F32)
        for h in range(RET_HEADS):
            lg = float(log_g[h])
            decay_ref[h] = jnp.where(diff >= 0, jnp.exp(jnp.maximum(diff, 0.0) * lg), 0.0)
            cross_ref[h] = jnp.exp((idx_v + 1.0) * lg)
            kdec_ref[h] = jnp.exp((chunk - 1.0 - idx_k) * lg)

    for h in range(RET_HEADS):
        q = qk_ref[:, h * RET_K_DIM:(h + 1) * RET_K_DIM]
        k = qk_ref[:, (RET_HEADS + h) * RET_K_DIM:(RET_HEADS + h + 1) * RET_K_DIM]
        v = v_ref[:, h * RET_V_DIM:(h + 1) * RET_V_DIM]
        gr = gr_ref[:, h * RET_V_DIM:(h + 1) * RET_V_DIM]
        scores = lax.dot_general(q, k, _NT, preferred_element_type=F32) * decay_ref[h]
        state = state_ref[h]
        o = (jnp.dot(scores.astype(BF16), v, preferred_element_type=F32)
             + jnp.dot(q, state.astype(BF16), preferred_element_type=F32) * cross_ref[h])
        k_w = (k.astype(F32) * kdec_ref[h]).astype(BF16)
        new_state = math.exp(chunk * float(log_g[h])) * state + lax.dot_general(
            k_w, v, (((0,), (0,)), ((), ())), preferred_element_type=F32)
        state_ref[h] = new_state
        o_ref[:, h * RET_V_DIM:(h + 1) * RET_V_DIM] = (_rms_norm(o) * _silu(gr.astype(F32))).astype(o_ref.dtype)

    @pl.when(c == pl.num_programs(1) - 1)
    def _():
        sfin_ref[...] = state_ref[...]


def _prompt_retention(qk, v, gr, *, chunk):
    bsz, seq, _ = qk.shape
    row_spec = pl.BlockSpec((None, chunk, D_MODEL), lambda b, c: (b, c, 0))
    state_shape = (RET_HEADS, RET_K_DIM, RET_V_DIM)
    return pl.pallas_call(
        functools.partial(_ret_kernel, chunk=chunk),
        grid=(bsz, seq // chunk),
        in_specs=[row_spec, row_spec, row_spec],
        out_specs=[row_spec, pl.BlockSpec((None,) + state_shape, lambda b, c: (b, 0, 0, 0))],
        out_shape=[
            jax.ShapeDtypeStruct((bsz, seq, RET_HEADS * RET_V_DIM), BF16),
            jax.ShapeDtypeStruct((bsz,) + state_shape, F32),
        ],
        scratch_shapes=[
            pltpu.VMEM(state_shape, F32),
            pltpu.VMEM((RET_HEADS, chunk, chunk), F32),
            pltpu.VMEM((RET_HEADS, chunk, RET_V_DIM), F32),
            pltpu.VMEM((RET_HEADS, chunk, RET_K_DIM), F32),
        ],
        compiler_params=_params("parallel", "arbitrary"),
        name="prompt_retention",
    )(qk, v, gr)


def _merge_kernel(x_ref, oa_ref, ob_ref, ga_ref, gb_ref, wa_ref, wb_ref, wo_ref, g_ref, b_ref, o_ref, *, alpha):
    a = jnp.dot(oa_ref[...].astype(BF16), wa_ref[...], preferred_element_type=F32)
    b = jnp.dot(ob_ref[...].astype(BF16), wb_ref[...], preferred_element_type=F32)
    m = jax.nn.sigmoid(ga_ref[...].astype(F32)) * a + jax.nn.sigmoid(gb_ref[...].astype(F32)) * b
    y = alpha * x_ref[...] + jnp.dot(m.astype(BF16), wo_ref[...], preferred_element_type=F32)
    o_ref[...] = _layer_norm(y, g_ref[...], b_ref[...])


def _merge_sublayer(x, oa, ob, ga, gb, w_a, w_b, w_o, g, b, *, alpha, tm):
    n = x.shape[0]
    row_spec = pl.BlockSpec((tm, D_MODEL), lambda i: (i, 0))
    w_spec = pl.BlockSpec((D_MODEL, D_MODEL), lambda i: (0, 0))
    vec_spec = pl.BlockSpec((1, D_MODEL), lambda i: (0, 0))
    return pl.pallas_call(
        functools.partial(_merge_kernel, alpha=alpha),
        grid=(n // tm,),
        in_specs=[row_spec] * 5 + [w_spec] * 3 + [vec_spec] * 2,
        out_specs=row_spec,
        out_shape=jax.ShapeDtypeStruct((n, D_MODEL), F32),
        compiler_params=_params("parallel"),
        name="merge_sublayer",
    )(x, oa, ob, ga, gb, w_a, w_b, w_o, g, b)


def _dec_attn_kernel(pt_ref, q_ref, slope_ref, *refs, t_dec, lam_init, n_pp):
    del pt_ref
    k_refs, v_refs = refs[:n_pp], refs[n_pp:2 * n_pp]
    kn_ref, vn_ref, lp_ref, g_ref, o_ref, mb_ref, mbn_ref, m_ref, l_ref, acc_ref = refs[2 * n_pp:]
    p = pl.program_id(1)
    n_rows = 2 * t_dec * DA_HEADS
    page_rows = PAGE_SIZE * DA_HEADS

    @pl.when(p == 0)
    def _():
        m_ref[...] = jnp.full_like(m_ref, NEG_BIG)
        l_ref[...] = jnp.zeros_like(l_ref)
        acc_ref[...] = jnp.zeros_like(acc_ref)
        row = lax.broadcasted_iota(jnp.int32, (n_rows, page_rows), 0)
        col = lax.broadcasted_iota(jnp.int32, (n_rows, page_rows), 1)
        same_head = ((col - row) & (DA_HEADS - 1)) == 0
        tok = (row // DA_HEADS) % t_dec
        rel = (col // DA_HEADS - tok - PAST_LEN).astype(F32)
        mb_ref[...] = jnp.where(same_head, slope_ref[...] * rel, NEG_BIG)
        rown = lax.broadcasted_iota(jnp.int32, (n_rows, PAGE_SIZE), 0)
        coln = lax.broadcasted_iota(jnp.int32, (n_rows, PAGE_SIZE), 1)
        tokq = (rown // DA_HEADS) % t_dec
        tokk = coln // DA_HEADS
        ok = (((coln - rown) & (DA_HEADS - 1)) == 0) & (tokk <= tokq)
        mbn_ref[...] = jnp.where(ok, slope_ref[...] * (tokk - tokq).astype(F32), NEG_BIG)

    def update(blocks):
        q = q_ref[...]
        scores = [lax.dot_general(q, kf.astype(BF16), _NT, preferred_element_type=F32) + mb
                  for kf, _, mb in blocks]
        m_prev = m_ref[...]
        m_new = m_prev
        for s in scores:
            m_new = jnp.maximum(m_new, jnp.max(s, axis=-1, keepdims=True))
        alpha = jnp.exp(m_prev - m_new)
        l_new = alpha * l_ref[...]
        acc_new = alpha * acc_ref[...]
        for s, (_, vf, _) in zip(scores, blocks):
            pr = jnp.exp(s - m_new)
            l_new = l_new + jnp.sum(pr, axis=-1, keepdims=True)
            acc_new = acc_new + jnp.dot(pr.astype(BF16), vf.astype(BF16), preferred_element_type=F32)
        l_ref[...] = l_new
        acc_ref[...] = acc_new
        m_ref[...] = m_new

    slope = slope_ref[...]
    update([(k_refs[r][...].reshape(page_rows, DA_V_DIM), v_refs[r][...].reshape(page_rows, DA_V_DIM),
             mb_ref[...] + slope * ((p * n_pp + r) * PAGE_SIZE).astype(F32)) for r in range(n_pp)])

    @pl.when(p == pl.num_programs(1) - 1)
    def _():
        update([(kn_ref[...], vn_ref[...], mbn_ref[...])])
        o = acc_ref[...] / l_ref[...]
        lam = _diff_lambda(lp_ref, lam_init)
        half = n_rows // 2
        d = o[0:half] - lam * o[half:n_rows]
        o_ref[...] = _rms_norm(d) * g_ref[...] * (1.0 - lam_init)


def _sample_attention(page_table, q_rows, cache_k, cache_v, k_new, v_new, lp, subln_g, *,
                      t_dec, lam_init, n_pp):
    n_dec, n_pages = page_table.shape
    n_rows = 2 * t_dec * DA_HEADS
    assert t_dec * DA_HEADS <= PAGE_SIZE and n_pages % n_pp == 0
    slope_rows = jnp.asarray(np.tile(_alibi_slopes(), 2 * t_dec).reshape(n_rows, 1))
    page_specs = [pl.BlockSpec((None, PAGE_SIZE, DA_HEADS, DA_V_DIM),
                               functools.partial(lambda s, p, pt, r: (pt[s, p * n_pp + r], 0, 0, 0), r=r))
                  for r in range(n_pp)]
    new_spec = pl.BlockSpec((None, PAGE_SIZE, DA_V_DIM), lambda s, p, pt: (s, 0, 0))
    return pl.pallas_call(
        functools.partial(_dec_attn_kernel, t_dec=t_dec, lam_init=lam_init, n_pp=n_pp),
        grid_spec=pltpu.PrefetchScalarGridSpec(
            num_scalar_prefetch=1,
            grid=(n_dec, n_pages // n_pp),
            in_specs=[
                pl.BlockSpec((None, n_rows, DA_V_DIM), lambda s, p, pt: (s, 0, 0)),
                pl.BlockSpec((n_rows, 1), lambda s, p, pt: (0, 0)),
                *page_specs, *page_specs, new_spec, new_spec,
                pl.BlockSpec((4, DA_HEAD_DIM), lambda s, p, pt: (0, 0)),
                pl.BlockSpec((1, DA_V_DIM), lambda s, p, pt: (0, 0)),
            ],
            out_specs=pl.BlockSpec((None, n_rows // 2, DA_V_DIM), lambda s, p, pt: (s, 0, 0)),
            scratch_shapes=[
                pltpu.VMEM((n_rows, PAGE_SIZE * DA_HEADS), F32),
                pltpu.VMEM((n_rows, PAGE_SIZE), F32),
                pltpu.VMEM((n_rows, 1), F32),
                pltpu.VMEM((n_rows, 1), F32),
                pltpu.VMEM((n_rows, DA_V_DIM), F32),
            ],
        ),
        out_shape=jax.ShapeDtypeStruct((n_dec, n_rows // 2, DA_V_DIM), F32),
        compiler_params=_params("parallel", "arbitrary"),
        name="sample_diff_attention",
    )(page_table, q_rows, slope_rows, *([cache_k] * n_pp), *([cache_v] * n_pp), k_new, v_new, lp, subln_g)


def _dec_ret_kernel(qk_ref, kt_ref, v_ref, gr_ref, st_ref, o_ref, snew_ref, *, t_dec, t_pad):
    log_g = _retention_log_decay()
    qk_all = qk_ref[...]
    v_all = v_ref[...]
    gr_all = gr_ref[...]
    tok = lax.broadcasted_iota(jnp.int32, (t_pad, 1), 0).astype(F32)
    for h in range(RET_HEADS):
        lg = float(log_g[h])
        q = qk_all[:, h * RET_K_DIM:(h + 1) * RET_K_DIM]
        k = qk_all[:, (RET_HEADS + h) * RET_K_DIM:(RET_HEADS + h + 1) * RET_K_DIM]
        v = v_all[:, h * RET_V_DIM:(h + 1) * RET_V_DIM]
        gr = gr_all[:, h * RET_V_DIM:(h + 1) * RET_V_DIM]
        kt = kt_ref[h]
        state = st_ref[h]
        o = jnp.dot(q.astype(BF16), state.astype(BF16), preferred_element_type=F32) * jnp.exp((tok + 1.0) * lg)
        new_state = math.exp(t_dec * lg) * state
        for j in range(t_dec):
            kj = kt[:, j:j + 1]
            vj = v[j:j + 1, :]
            s_j = jnp.sum(q * k[j:j + 1, :], axis=-1, keepdims=True)
            d_j = jnp.where(tok >= j, jnp.exp(jnp.maximum(tok - j, 0.0) * lg), 0.0)
            o = o + (s_j * d_j) * vj
            new_state = new_state + (kj * math.exp((t_dec - 1.0 - j) * lg)) * vj
        snew_ref[h] = new_state
        o_ref[:, h * RET_V_DIM:(h + 1) * RET_V_DIM] = _rms_norm(o) * _silu(gr)


def _sample_retention(qk, kt, v, gr, state, *, t_dec):
    n_dec, t_pad, _ = qk.shape
    return pl.pallas_call(
        functools.partial(_dec_ret_kernel, t_dec=t_dec, t_pad=t_pad),
        grid=(n_dec,),
        in_specs=[
            pl.BlockSpec((None, t_pad, 2 * RET_HEADS * RET_K_DIM), lambda s: (s, 0, 0)),
            pl.BlockSpec((None, RET_HEADS, RET_K_DIM, t_pad), lambda s: (s, 0, 0, 0)),
            pl.BlockSpec((None, t_pad, RET_HEADS * RET_V_DIM), lambda s: (s, 0, 0)),
            pl.BlockSpec((None, t_pad, RET_HEADS * RET_V_DIM), lambda s: (s, 0, 0)),
            pl.BlockSpec((None, RET_HEADS, RET_K_DIM, RET_V_DIM), lambda s: (s, 0, 0, 0)),
        ],
        out_specs=[
            pl.BlockSpec((None, t_pad, RET_HEADS * RET_V_DIM), lambda s: (s, 0, 0)),
            pl.BlockSpec((None, RET_HEADS, RET_K_DIM, RET_V_DIM), lambda s: (s, 0, 0, 0)),
        ],
        out_shape=[
            jax.ShapeDtypeStruct((n_dec, t_pad, RET_HEADS * RET_V_DIM), F32),
            jax.ShapeDtypeStruct(state.shape, F32),
        ],
        compiler_params=_params("parallel"),
        name="sample_retention",
    )(qk, kt, v, gr, state)


def _sample_attention_inputs(q, k, v, n_dec, t_dec):
    qh = q.reshape(n_dec, t_dec, DA_HEADS, DA_V_DIM) * (DA_HEAD_DIM ** -0.5)
    first = jnp.arange(DA_V_DIM) < DA_HEAD_DIM
    q_rows = jnp.stack([jnp.where(first, qh, 0.0), jnp.where(first, 0.0, qh)], axis=1)
    q_rows = q_rows.reshape(n_dec, 2 * t_dec * DA_HEADS, DA_V_DIM).astype(BF16)
    pad = ((0, 0), (0, PAGE_SIZE - t_dec * DA_HEADS), (0, 0))
    k_new = jnp.pad(k.reshape(n_dec, t_dec * DA_HEADS, DA_V_DIM), pad)
    v_new = jnp.pad(v.reshape(n_dec, t_dec * DA_HEADS, DA_V_DIM), pad)
    return q_rows, k_new, v_new


def kernel(x_prompt, x_sample, cache_k, cache_v, state_ret, page_table, ln_g, ln_b, ffn1_w_gu, ffn1_w_down,
           w_in, da_lambda, da_subln_g, w_branch_a, w_branch_b, w_out, ffn2_w_gu, ffn2_w_down):
    depth = w_in.shape[0]
    bsz, seq, _ = x_prompt.shape
    n_dec, t_dec, _ = x_sample.shape
    alpha = (2.0 * depth) ** 0.25
    t_pad = 8
    tm_p, tm_s = 512, n_dec * t_dec
    tq = 256

    xp = x_prompt.reshape(bsz * seq, D_MODEL)
    xs = x_sample.reshape(n_dec * t_dec, D_MODEL)
    qkr_scale = jnp.concatenate([jnp.ones((RET_HEADS * RET_K_DIM,), F32),
                                 jnp.full((RET_HEADS * RET_K_DIM,), RET_K_DIM ** -0.5, F32)]).reshape(1, D_MODEL)
    outs = [[] for _ in range(6)]
    for l in range(depth):
        lam_init = _lambda_init(l)
        w1_gu, w1_d = ffn1_w_gu[l].astype(BF16), ffn1_w_down[l].astype(BF16)
        w2_gu, w2_d = ffn2_w_gu[l].astype(BF16), ffn2_w_down[l].astype(BF16)
        w_in_l = w_in[l].astype(BF16)
        w_a, w_b, w_o = (w[l].astype(BF16) for w in (w_branch_a, w_branch_b, w_out))
        ln = [(ln_g[l, s].reshape(1, D_MODEL), ln_b[l, s].reshape(1, D_MODEL)) for s in range(3)]
        lp = da_lambda[l].astype(F32)
        subln = da_subln_g[l].reshape(1, DA_V_DIM)

        xp = _ffn_sublayer(xp, w1_gu, w1_d, *ln[0], alpha=alpha, tm=tm_p)
        proj = _input_projection(xp, w_in_l, qkr_scale, act_dtype=BF16, tm=tm_p, tn=512)
        k32, v32 = proj[N_GROUPS], proj[N_GROUPS + 1]
        as_seq = lambda t: t.reshape(bsz, seq, D_MODEL)
        vt = proj[G_VA].reshape(bsz, seq // tq, tq, DA_HEADS, DA_V_DIM).transpose(0, 3, 1, 4, 2)
        vt = jnp.concatenate([vt, jnp.ones((bsz, DA_HEADS, seq // tq, ONES_ROWS, tq), BF16)], axis=3)
        oa = _prompt_attention(as_seq(proj[G_QA]), as_seq(proj[G_KA]), vt, lp, subln.reshape(DA_V_DIM, 1),
                               lam_init=lam_init, tq=tq)
        ob, s_fin = _prompt_retention(as_seq(proj[G_QKR]), as_seq(proj[G_VR]), as_seq(proj[G_GR]), chunk=128)
        xp = _merge_sublayer(xp, oa.reshape(bsz * seq, D_MODEL), ob.reshape(bsz * seq, D_MODEL),
                             proj[G_GA], proj[G_GB], w_a, w_b, w_o, *ln[1], alpha=alpha, tm=tm_p)
        xp = _ffn_sublayer(xp, w2_gu, w2_d, *ln[2], alpha=alpha, tm=tm_p)
        outs[0].append(k32.reshape(bsz, seq, DA_HEADS, DA_V_DIM))
        outs[1].append(v32.reshape(bsz, seq, DA_HEADS, DA_V_DIM))
        outs[2].append(s_fin)

        xs = _ffn_sublayer(xs, w1_gu, w1_d, *ln[0], alpha=alpha, tm=tm_s)
        proj = _input_projection(xs, w_in_l, qkr_scale, act_dtype=F32, tm=tm_s, tn=512)
        k32, v32 = proj[N_GROUPS], proj[N_GROUPS + 1]
        q_rows, k_new, v_new = _sample_attention_inputs(proj[G_QA], k32, v32, n_dec, t_dec)
        oa = _sample_attention(page_table, q_rows, cache_k[l], cache_v[l], k_new, v_new, lp, subln,
                               t_dec=t_dec, lam_init=lam_init, n_pp=16)
        tok_pad = lambda t: jnp.pad(t.reshape(n_dec, t_dec, -1), ((0, 0), (0, t_pad - t_dec), (0, 0)))
        qk_r = tok_pad(proj[G_QKR])
        kt_r = qk_r[:, :, RET_HEADS * RET_K_DIM:].reshape(n_dec, t_pad, RET_HEADS, RET_K_DIM)
        kt_r = kt_r.transpose(0, 2, 3, 1)
        ob, s_new = _sample_retention(qk_r, kt_r, tok_pad(proj[G_VR]), tok_pad(proj[G_GR]), state_ret[l],
                                      t_dec=t_dec)
        ob = ob[:, :t_dec].reshape(n_dec * t_dec, D_MODEL)
        xs = _merge_sublayer(xs, oa.reshape(n_dec * t_dec, D_MODEL), ob, proj[G_GA], proj[G_GB],
                             w_a, w_b, w_o, *ln[1], alpha=alpha, tm=tm_s)
        xs = _ffn_sublayer(xs, w2_gu, w2_d, *ln[2], alpha=alpha, tm=tm_s)
        outs[3].append(k32.reshape(n_dec, t_dec, DA_HEADS, DA_V_DIM))
        outs[4].append(v32.reshape(n_dec, t_dec, DA_HEADS, DA_V_DIM))
        outs[5].append(s_new)

    k_p, v_p, s_p, k_s, v_s, s_s = (jnp.stack(o) for o in outs)
    return (xp.reshape(bsz, seq, D_MODEL), xs.reshape(n_dec, t_dec, D_MODEL), k_p, v_p, s_p, k_s, v_s, s_s)
```

```python
import functools
import math

import numpy as np
import jax
import jax.numpy as jnp
from jax import lax
from jax.experimental import pallas as pl
from jax.experimental.pallas import tpu as pltpu

D_MODEL = 1024
PAST_LEN = 8192
PAGE_SIZE = 128
DA_HEADS = 8
DA_HEAD_DIM = 64
DA_V_DIM = 2 * DA_HEAD_DIM
RET_HEADS = 4
RET_K_DIM = 128
RET_V_DIM = 256
D_FF = 2816
LN_EPS = 1e-5
N_GROUPS = 8
G_QA, G_KA, G_VA, G_QKR, G_VR, G_GR, G_GA, G_GB = range(N_GROUPS)

VMEM_LIMIT_BYTES = 48 * 1024 * 1024
NEG_BIG = -1e30
ONES_ROWS = 16
MXU_DIM = 256
FF_CHUNKS = tuple((s, min(3 * MXU_DIM, D_FF - s)) for s in range(0, D_FF, 3 * MXU_DIM))
assert all(size % MXU_DIM == 0 for _, size in FF_CHUNKS)
PROJ_CHUNK = 2 * MXU_DIM

F32 = jnp.float32
BF16 = jnp.bfloat16
_NT = (((1,), (1,)), ((), ()))


def _alibi_slopes():
    return (2.0 ** (-8.0 * np.arange(1, DA_HEADS + 1) / DA_HEADS)).astype(np.float32)


def _retention_log_decay():
    return np.log(1.0 - 2.0 ** (-5.0 - np.arange(RET_HEADS))).astype(np.float32)


def _lambda_init(layer):
    return 0.8 - 0.6 * math.exp(-0.3 * layer)


def _params(*semantics):
    return pltpu.CompilerParams(dimension_semantics=semantics, vmem_limit_bytes=VMEM_LIMIT_BYTES)


def _layer_norm(y, g, b):
    mu = jnp.mean(y, axis=-1, keepdims=True)
    d = y - mu
    var = jnp.mean(d * d, axis=-1, keepdims=True)
    return d * lax.rsqrt(var + LN_EPS) * g + b


def _rms_norm(o):
    return o * lax.rsqrt(jnp.mean(o * o, axis=-1, keepdims=True) + LN_EPS)


def _silu(a):
    return a * jax.nn.sigmoid(a)


def _diff_lambda(lp_ref, lam_init):
    lp = lp_ref[...]
    a = jnp.sum(lp[0:1] * lp[1:2], axis=-1, keepdims=True)
    b = jnp.sum(lp[2:3] * lp[3:4], axis=-1, keepdims=True)
    return jnp.exp(a) - jnp.exp(b) + lam_init


def _ffn_kernel(x_ref, wgu_ref, wd_ref, g_ref, b_ref, o_ref, *, alpha):
    x = x_ref[...]
    xb = x.astype(BF16)
    acc = None
    for start, size in FF_CHUNKS:
        gate = jnp.dot(xb, wgu_ref[:, start:start + size], preferred_element_type=F32)
        up = jnp.dot(xb, wgu_ref[:, D_FF + start:D_FF + start + size], preferred_element_type=F32)
        h = (_silu(gate) * up).astype(BF16)
        part = jnp.dot(h, wd_ref[start:start + size, :], preferred_element_type=F32)
        acc = part if acc is None else acc + part
    o_ref[...] = _layer_norm(alpha * x + 0.5 * acc, g_ref[...], b_ref[...])


def _ffn_sublayer(x, w_gu, w_down, g, b, *, alpha, tm):
    n = x.shape[0]
    resident = functools.partial(pl.BlockSpec, index_map=lambda i: (0, 0), pipeline_mode=pl.Buffered(1))
    return pl.pallas_call(
        functools.partial(_ffn_kernel, alpha=alpha),
        grid=(n // tm,),
        in_specs=[
            pl.BlockSpec((tm, D_MODEL), lambda i: (i, 0)),
            resident((D_MODEL, 2 * D_FF)),
            resident((D_FF, D_MODEL)),
            resident((1, D_MODEL)),
            resident((1, D_MODEL)),
        ],
        out_specs=pl.BlockSpec((tm, D_MODEL), lambda i: (i, 0)),
        out_shape=jax.ShapeDtypeStruct((n, D_MODEL), F32),
        compiler_params=_params("parallel"),
        name="ffn_sublayer",
    )(x, w_gu, w_down, g, b)


def _proj_kernel(x_ref, w_ref, scale_ref, *refs, emit_vt):
    out_refs = refs[:N_GROUPS]
    k32_ref, v32_ref = refs[N_GROUPS:N_GROUPS + 2]
    xb = x_ref[...].astype(BF16)
    for grp in range(N_GROUPS):
        for start in range(0, D_MODEL, PROJ_CHUNK):
            cols = slice(start, start + PROJ_CHUNK)
            y = jnp.dot(xb, w_ref[:, grp * D_MODEL + start:grp * D_MODEL + start + PROJ_CHUNK],
                        preferred_element_type=F32)
            if grp == G_QKR:
                y = y * scale_ref[:, cols]
            out_refs[grp][:, cols] = y.astype(out_refs[grp].dtype)
            if grp == G_KA:
                k32_ref[:, cols] = y
            if grp == G_VA:
                v32_ref[:, cols] = y
                if emit_vt:
                    vt_ref = refs[N_GROUPS + 2]
                    for hh in range(PROJ_CHUNK // DA_V_DIM):
                        head = start // DA_V_DIM + hh
                        vt_ref[head, 0:DA_V_DIM, :] = y[:, hh * DA_V_DIM:(hh + 1) * DA_V_DIM].T.astype(BF16)
                        vt_ref[head, DA_V_DIM:DA_V_DIM + ONES_ROWS, :] = jnp.ones(
                            (ONES_ROWS, y.shape[0]), BF16)


def _input_projection(x, w_in, qkr_scale, *, act_dtype, tm, vt_batch=None):
    n = x.shape[0]
    resident = functools.partial(pl.BlockSpec, index_map=lambda i: (0, 0), pipeline_mode=pl.Buffered(1))
    row_spec = pl.BlockSpec((tm, D_MODEL), lambda i: (i, 0))
    out_specs = [row_spec] * (N_GROUPS + 2)
    out_shape = [jax.ShapeDtypeStruct((n, D_MODEL), act_dtype)] * N_GROUPS + [
        jax.ShapeDtypeStruct((n, D_MODEL), F32)] * 2
    if vt_batch is not None:
        nk = n // vt_batch // tm
        vt_rows = DA_V_DIM + ONES_ROWS
        out_specs.append(pl.BlockSpec((None, DA_HEADS, None, vt_rows, tm), lambda i: (i // nk, 0, i % nk, 0, 0)))
        out_shape.append(jax.ShapeDtypeStruct((vt_batch, DA_HEADS, nk, vt_rows, tm), BF16))
    return pl.pallas_call(
        functools.partial(_proj_kernel, emit_vt=vt_batch is not None),
        grid=(n // tm,),
        in_specs=[row_spec, resident((D_MODEL, N_GROUPS * D_MODEL)), resident((1, D_MODEL))],
        out_specs=out_specs,
        out_shape=out_shape,
        compiler_params=_params("parallel"),
        name="input_projection",
    )(x, w_in, qkr_scale)


HEADS_PER_STEP = 2


def _attn_kernel(slopes_ref, q_ref, k_ref, vt_ref, lp_ref, g_ref, o_ref,
                 qst_ref, kfeat_ref, mask_ref, s_ref, p_ref, alpha_ref, m_ref, acc_ref, *, tq, lam_init):
    hp = pl.program_id(1)
    i = pl.program_id(2)
    heads = range(HEADS_PER_STEP)
    slopes = [slopes_ref[hp * HEADS_PER_STEP + hh] for hh in heads]

    @pl.when(i == 0)
    def _():
        sub = lax.broadcasted_iota(jnp.int32, (DA_V_DIM, 2 * tq), 0)
        qrel = (lax.broadcasted_iota(jnp.int32, (DA_V_DIM, 2 * tq), 1) & (tq - 1)).astype(F32)
        for hh in heads:
            feat = jnp.where(sub == 0, slopes[hh], jnp.where(sub == 1, -slopes[hh] * qrel, 0.0))
            qst_ref[hh, DA_V_DIM:2 * DA_V_DIM, :] = feat.astype(BF16)
        lane = lax.broadcasted_iota(jnp.int32, (tq, DA_V_DIM), 1)
        krel = lax.broadcasted_iota(jnp.int32, (tq, DA_V_DIM), 0).astype(F32)
        kfeat_ref[...] = jnp.where(lane == 0, krel, jnp.where(lane == 1, 1.0, 0.0)).astype(BF16)
        key = lax.broadcasted_iota(jnp.int32, (tq, 2 * tq), 0)
        qry = lax.broadcasted_iota(jnp.int32, (tq, 2 * tq), 1) & (tq - 1)
        mask_ref[...] = jnp.where(key <= qry, 0.0, NEG_BIG)

    for hh in heads:
        q = q_ref[:, hh * DA_V_DIM:(hh + 1) * DA_V_DIM]
        qt = (q.astype(F32) * (DA_HEAD_DIM ** -0.5)).T.astype(BF16)
        sub = lax.broadcasted_iota(jnp.int32, qt.shape, 0)
        zero = jnp.zeros_like(qt)
        qst_ref[hh, 0:DA_V_DIM, 0:tq] = jnp.where(sub < DA_HEAD_DIM, qt, zero)
        qst_ref[hh, 0:DA_V_DIM, tq:2 * tq] = jnp.where(sub >= DA_HEAD_DIM, qt, zero)

    m_ref[...] = jnp.full_like(m_ref, NEG_BIG)
    acc_ref[...] = jnp.zeros_like(acc_ref)

    def block_of(r):
        if isinstance(r, int):
            return i if r == 0 else r - 1
        return jnp.where(r == 0, i, r - 1)

    def qk(r, slot):
        j = block_of(r)
        for hh in heads:
            k = k_ref[pl.ds(j * tq, tq), hh * DA_V_DIM:(hh + 1) * DA_V_DIM]
            k_aug = jnp.concatenate([k, kfeat_ref[...]], axis=1)
            s = jnp.dot(k_aug, qst_ref[hh], preferred_element_type=F32)
            if isinstance(r, int) and r == 0:
                s = s + mask_ref[...]
            s_ref[slot, hh] = s

    def softmax(r, slot):
        j = block_of(r)
        for hh in heads:
            s = s_ref[slot, hh]
            c = slopes[hh] * ((j - i) * tq).astype(F32)
            m_prev = m_ref[hh]
            m_new = jnp.maximum(m_prev, jnp.max(s, axis=0, keepdims=True) + c)
            alpha_ref[slot, hh] = jnp.exp(m_prev - m_new)
            p_ref[slot, hh] = jnp.exp(s - (m_new - c)).astype(BF16)
            m_ref[hh] = m_new

    def pv(r, slot):
        j = block_of(r)
        for hh in heads:
            acc_ref[hh] = alpha_ref[slot, hh] * acc_ref[hh] + jnp.dot(
                vt_ref[hh, j], p_ref[slot, hh], preferred_element_type=F32)

    def tick(u, parity, *, with_pv=True, with_qk=True):
        if with_pv:
            pv(u - 2, parity)
        softmax(u - 1, 1 - parity)
        if with_qk:
            qk(u, parity)

    def two_ticks(t, carry):
        u = 2 * t
        softmax(u - 1, 1)
        qk(u, 0)
        qk(u + 1, 1)
        pv(u - 2, 0)
        softmax(u, 0)
        pv(u - 1, 1)
        return carry

    n = i + 1
    qk(0, 0)

    @pl.when(n >= 2)
    def _():
        tick(1, 1, with_pv=False)

    lax.fori_loop(1, jnp.maximum(n - 2, 0) // 2 + 1, two_ticks, 0)

    @pl.when((n >= 2) & (n % 2 == 0))
    def _():
        tick(n, 0, with_qk=False)
        pv(n - 1, 1)

    @pl.when((n >= 3) & (n % 2 == 1))
    def _():
        tick(n - 1, 0)
        tick(n, 1, with_qk=False)
        pv(n - 1, 0)

    @pl.when(n == 1)
    def _():
        softmax(0, 0)
        pv(0, 0)

    lam = _diff_lambda(lp_ref, lam_init)
    for hh in heads:
        o = acc_ref[hh, 0:DA_V_DIM, :] / acc_ref[hh, DA_V_DIM:DA_V_DIM + 1, :]
        d = o[:, 0:tq] - lam * o[:, tq:2 * tq]
        r = d * lax.rsqrt(jnp.mean(d * d, axis=0, keepdims=True) + LN_EPS) * g_ref[...] * (1.0 - lam_init)
        o_ref[:, hh * DA_V_DIM:(hh + 1) * DA_V_DIM] = r.T.astype(o_ref.dtype)


def _prompt_attention(q, k, vt, lp, subln_g_col, *, lam_init, tq):
    bsz, seq, _ = q.shape
    assert tq & (tq - 1) == 0 and tq <= 256 and seq % tq == 0 and DA_HEADS % HEADS_PER_STEP == 0
    nk = seq // tq
    vt_rows = DA_V_DIM + ONES_ROWS
    hw = HEADS_PER_STEP * DA_V_DIM
    return pl.pallas_call(
        functools.partial(_attn_kernel, tq=tq, lam_init=lam_init),
        grid=(bsz, DA_HEADS // HEADS_PER_STEP, nk),
        in_specs=[
            pl.BlockSpec(memory_space=pltpu.SMEM),
            pl.BlockSpec((None, tq, hw), lambda b, h, i: (b, i, h)),
            pl.BlockSpec((None, seq, hw), lambda b, h, i: (b, 0, h)),
            pl.BlockSpec((None, HEADS_PER_STEP, nk, vt_rows, tq), lambda b, h, i: (b, h, 0, 0, 0)),
            pl.BlockSpec((4, DA_HEAD_DIM), lambda b, h, i: (0, 0)),
            pl.BlockSpec((DA_V_DIM, 1), lambda b, h, i: (0, 0)),
        ],
        out_specs=pl.BlockSpec((None, tq, hw), lambda b, h, i: (b, i, h)),
        out_shape=jax.ShapeDtypeStruct((bsz, seq, DA_HEADS * DA_V_DIM), BF16),
        scratch_shapes=[
            pltpu.VMEM((HEADS_PER_STEP, 2 * DA_V_DIM, 2 * tq), BF16),
            pltpu.VMEM((tq, DA_V_DIM), BF16),
            pltpu.VMEM((tq, 2 * tq), F32),
            pltpu.VMEM((2, HEADS_PER_STEP, tq, 2 * tq), F32),
            pltpu.VMEM((2, HEADS_PER_STEP, tq, 2 * tq), BF16),
            pltpu.VMEM((2, HEADS_PER_STEP, 1, 2 * tq), F32),
            pltpu.VMEM((HEADS_PER_STEP, 1, 2 * tq), F32),
            pltpu.VMEM((HEADS_PER_STEP, vt_rows, 2 * tq), F32),
        ],
        compiler_params=_params("parallel", "parallel", "arbitrary"),
        name="prompt_diff_attention",
    )(jnp.asarray(_alibi_slopes()), q, k, vt, lp, subln_g_col)


def _ret_kernel(qk_ref, v_ref, gr_ref, o_ref, sfin_ref, state_ref, decay_ref, cross_ref, kdec_ref, *, chunk):
    c = pl.program_id(1)
    log_g = _retention_log_decay()

    @pl.when(c == 0)
    def _():
        state_ref[...] = jnp.zeros_like(state_ref)
        row = lax.broadcasted_iota(jnp.int32, (chunk, chunk), 0)
        col = lax.broadcasted_iota(jnp.int32, (chunk, chunk), 1)
        diff = (row - col).astype(F32)
        idx_v = lax.broadcasted_iota(jnp.int32, (chunk, RET_V_DIM), 0).astype(F32)
        idx_k = lax.broadcasted_iota(jnp.int32, (chunk, RET_K_DIM), 0).astype(F32)
        for h in range(RET_HEADS):
            lg = float(log_g[h])
            decay_ref[h] = jnp.where(diff >= 0, jnp.exp(jnp.maximum(diff, 0.0) * lg), 0.0)
            cross_ref[h] = jnp.exp((idx_v + 1.0) * lg)
            kdec_ref[h] = jnp.exp((chunk - 1.0 - idx_k) * lg)

    for h in range(RET_HEADS):
        q = qk_ref[:, h * RET_K_DIM:(h + 1) * RET_K_DIM]
        k = qk_ref[:, (RET_HEADS + h) * RET_K_DIM:(RET_HEADS + h + 1) * RET_K_DIM]
        v = v_ref[:, h * RET_V_DIM:(h + 1) * RET_V_DIM]
        gr = gr_ref[:, h * RET_V_DIM:(h + 1) * RET_V_DIM]
        scores = lax.dot_general(q, k, _NT, preferred_element_type=F32) * decay_ref[h]
        state = state_ref[h]
        o = (jnp.dot(scores.astype(BF16), v, preferred_element_type=F32)
             + jnp.dot(q, state.astype(BF16), preferred_element_type=F32) * cross_ref[h])
        k_w = (k.astype(F32) * kdec_ref[h]).astype(BF16)
        new_state = math.exp(chunk * float(log_g[h])) * state + lax.dot_general(
            k_w, v, (((0,), (0,)), ((), ())), preferred_element_type=F32)
        state_ref[h] = new_state
        o_ref[:, h * RET_V_DIM:(h + 1) * RET_V_DIM] = (_rms_norm(o) * _silu(gr.astype(F32))).astype(o_ref.dtype)

    @pl.when(c == pl.num_programs(1) - 1)
    def _():
        sfin_ref[...] = state_ref[...]


def _prompt_retention(qk, v, gr, *, chunk):
    bsz, seq, _ = qk.shape
    row_spec = pl.BlockSpec((None, chunk, D_MODEL), lambda b, c: (b, c, 0))
    state_shape = (RET_HEADS, RET_K_DIM, RET_V_DIM)
    return pl.pallas_call(
        functools.partial(_ret_kernel, chunk=chunk),
        grid=(bsz, seq // chunk),
        in_specs=[row_spec, row_spec, row_spec],
        out_specs=[row_spec, pl.BlockSpec((None,) + state_shape, lambda b, c: (b, 0, 0, 0))],
        out_shape=[
            jax.ShapeDtypeStruct((bsz, seq, RET_HEADS * RET_V_DIM), BF16),
            jax.ShapeDtypeStruct((bsz,) + state_shape, F32),
        ],
        scratch_shapes=[
            pltpu.VMEM(state_shape, F32),
            pltpu.VMEM((RET_HEADS, chunk, chunk), F32),
            pltpu.VMEM((RET_HEADS, chunk, RET_V_DIM), F32),
            pltpu.VMEM((RET_HEADS, chunk, RET_K_DIM), F32),
        ],
        compiler_params=_params("parallel", "arbitrary"),
        name="prompt_retention",
    )(qk, v, gr)


def _merge_kernel(x_ref, oa_ref, ob_ref, ga_ref, gb_ref, wa_ref, wb_ref, wo_ref, g_ref, b_ref, o_ref, *, alpha):
    a = jnp.dot(oa_ref[...].astype(BF16), wa_ref[...], preferred_element_type=F32)
    b = jnp.dot(ob_ref[...].astype(BF16), wb_ref[...], preferred_element_type=F32)
    m = jax.nn.sigmoid(ga_ref[...].astype(F32)) * a + jax.nn.sigmoid(gb_ref[...].astype(F32)) * b
    y = alpha * x_ref[...] + jnp.dot(m.astype(BF16), wo_ref[...], preferred_element_type=F32)
    o_ref[...] = _layer_norm(y, g_ref[...], b_ref[...])


def _merge_sublayer(x, oa, ob, ga, gb, w_a, w_b, w_o, g, b, *, alpha, tm):
    n = x.shape[0]
    row_spec = pl.BlockSpec((tm, D_MODEL), lambda i: (i, 0))
    w_spec = pl.BlockSpec((D_MODEL, D_MODEL), lambda i: (0, 0))
    vec_spec = pl.BlockSpec((1, D_MODEL), lambda i: (0, 0))
    return pl.pallas_call(
        functools.partial(_merge_kernel, alpha=alpha),
        grid=(n // tm,),
        in_specs=[row_spec] * 5 + [w_spec] * 3 + [vec_spec] * 2,
        out_specs=row_spec,
        out_shape=jax.ShapeDtypeStruct((n, D_MODEL), F32),
        compiler_params=_params("parallel"),
        name="merge_sublayer",
    )(x, oa, ob, ga, gb, w_a, w_b, w_o, g, b)


def _dec_attn_kernel(pt_ref, q_ref, slope_ref, *refs, t_dec, lam_init, n_pp):
    del pt_ref
    k_refs, v_refs = refs[:n_pp], refs[n_pp:2 * n_pp]
    kn_ref, vn_ref, lp_ref, g_ref, o_ref, mb_ref, mbn_ref, m_ref, l_ref, acc_ref = refs[2 * n_pp:]
    p = pl.program_id(1)
    n_rows = 2 * t_dec * DA_HEADS
    page_rows = PAGE_SIZE * DA_HEADS

    @pl.when(p == 0)
    def _():
        m_ref[...] = jnp.full_like(m_ref, NEG_BIG)
        l_ref[...] = jnp.zeros_like(l_ref)
        acc_ref[...] = jnp.zeros_like(acc_ref)
        row = lax.broadcasted_iota(jnp.int32, (n_rows, page_rows), 0)
        col = lax.broadcasted_iota(jnp.int32, (n_rows, page_rows), 1)
        same_head = ((col - row) & (DA_HEADS - 1)) == 0
        tok = (row // DA_HEADS) % t_dec
        rel = (col // DA_HEADS - tok - PAST_LEN).astype(F32)
        mb_ref[...] = jnp.where(same_head, slope_ref[...] * rel, NEG_BIG)
        rown = lax.broadcasted_iota(jnp.int32, (n_rows, PAGE_SIZE), 0)
        coln = lax.broadcasted_iota(jnp.int32, (n_rows, PAGE_SIZE), 1)
        tokq = (rown // DA_HEADS) % t_dec
        tokk = coln // DA_HEADS
        ok = (((coln - rown) & (DA_HEADS - 1)) == 0) & (tokk <= tokq)
        mbn_ref[...] = jnp.where(ok, slope_ref[...] * (tokk - tokq).astype(F32), NEG_BIG)

    def update(blocks):
        q = q_ref[...]
        scores = [lax.dot_general(q, kf.astype(BF16), _NT, preferred_element_type=F32) + mb
                  for kf, _, mb in blocks]
        m_prev = m_ref[...]
        m_new = m_prev
        for s in scores:
            m_new = jnp.maximum(m_new, jnp.max(s, axis=-1, keepdims=True))
        alpha = jnp.exp(m_prev - m_new)
        l_new = alpha * l_ref[...]
        acc_new = alpha * acc_ref[...]
        for s, (_, vf, _) in zip(scores, blocks):
            pr = jnp.exp(s - m_new)
            l_new = l_new + jnp.sum(pr, axis=-1, keepdims=True)
            acc_new = acc_new + jnp.dot(pr.astype(BF16), vf.astype(BF16), preferred_element_type=F32)
        l_ref[...] = l_new
        acc_ref[...] = acc_new
        m_ref[...] = m_new

    slope = slope_ref[...]
    update([(k_refs[r][...].reshape(page_rows, DA_V_DIM), v_refs[r][...].reshape(page_rows, DA_V_DIM),
             mb_ref[...] + slope * ((p * n_pp + r) * PAGE_SIZE).astype(F32)) for r in range(n_pp)])

    @pl.when(p == pl.num_programs(1) - 1)
    def _():
        update([(kn_ref[...], vn_ref[...], mbn_ref[...])])
        o = acc_ref[...] / l_ref[...]
        lam = _diff_lambda(lp_ref, lam_init)
        half = n_rows // 2
        d = o[0:half] - lam * o[half:n_rows]
        o_ref[...] = _rms_norm(d) * g_ref[...] * (1.0 - lam_init)


def _sample_attention(page_table, q_rows, cache_k, cache_v, k_new, v_new, lp, subln_g, *,
                      t_dec, lam_init, n_pp):
    n_dec, n_pages = page_table.shape
    n_rows = 2 * t_dec * DA_HEADS
    assert t_dec * DA_HEADS <= PAGE_SIZE and n_pages % n_pp == 0
    slope_rows = jnp.asarray(np.tile(_alibi_slopes(), 2 * t_dec).reshape(n_rows, 1))
    page_specs = [pl.BlockSpec((None, PAGE_SIZE, DA_HEADS, DA_V_DIM),
                               functools.partial(lambda s, p, pt, r: (pt[s, p * n_pp + r], 0, 0, 0), r=r))
                  for r in range(n_pp)]
    new_spec = pl.BlockSpec((None, PAGE_SIZE, DA_V_DIM), lambda s, p, pt: (s, 0, 0))
    return pl.pallas_call(
        functools.partial(_dec_attn_kernel, t_dec=t_dec, lam_init=lam_init, n_pp=n_pp),
        grid_spec=pltpu.PrefetchScalarGridSpec(
            num_scalar_prefetch=1,
            grid=(n_dec, n_pages // n_pp),
            in_specs=[
                pl.BlockSpec((None, n_rows, DA_V_DIM), lambda s, p, pt: (s, 0, 0)),
                pl.BlockSpec((n_rows, 1), lambda s, p, pt: (0, 0)),
                *page_specs, *page_specs, new_spec, new_spec,
                pl.BlockSpec((4, DA_HEAD_DIM), lambda s, p, pt: (0, 0)),
                pl.BlockSpec((1, DA_V_DIM), lambda s, p, pt: (0, 0)),
            ],
            out_specs=pl.BlockSpec((None, n_rows // 2, DA_V_DIM), lambda s, p, pt: (s, 0, 0)),
            scratch_shapes=[
                pltpu.VMEM((n_rows, PAGE_SIZE * DA_HEADS), F32),
                pltpu.VMEM((n_rows, PAGE_SIZE), F32),
                pltpu.VMEM((n_rows, 1), F32),
                pltpu.VMEM((n_rows, 1), F32),
                pltpu.VMEM((n_rows, DA_V_DIM), F32),
            ],
        ),
        out_shape=jax.ShapeDtypeStruct((n_dec, n_rows // 2, DA_V_DIM), F32),
        compiler_params=_params("parallel", "arbitrary"),
        name="sample_diff_attention",
    )(page_table, q_rows, slope_rows, *([cache_k] * n_pp), *([cache_v] * n_pp), k_new, v_new, lp, subln_g)


def _dec_ret_kernel(qk_ref, kt_ref, v_ref, gr_ref, st_ref, o_ref, snew_ref, *, t_dec, t_pad):
    log_g = _retention_log_decay()
    qk_all = qk_ref[...]
    v_all = v_ref[...]
    gr_all = gr_ref[...]
    tok = lax.broadcasted_iota(jnp.int32, (t_pad, 1), 0).astype(F32)
    for h in range(RET_HEADS):
        lg = float(log_g[h])
        q = qk_all[:, h * RET_K_DIM:(h + 1) * RET_K_DIM]
        k = qk_all[:, (RET_HEADS + h) * RET_K_DIM:(RET_HEADS + h + 1) * RET_K_DIM]
        v = v_all[:, h * RET_V_DIM:(h + 1) * RET_V_DIM]
        gr = gr_all[:, h * RET_V_DIM:(h + 1) * RET_V_DIM]
        kt = kt_ref[h]
        state = st_ref[h]
        o = jnp.dot(q.astype(BF16), state.astype(BF16), preferred_element_type=F32) * jnp.exp((tok + 1.0) * lg)
        new_state = math.exp(t_dec * lg) * state
        for j in range(t_dec):
            kj = kt[:, j:j + 1]
            vj = v[j:j + 1, :]
            s_j = jnp.sum(q * k[j:j + 1, :], axis=-1, keepdims=True)
            d_j = jnp.where(tok >= j, jnp.exp(jnp.maximum(tok - j, 0.0) * lg), 0.0)
            o = o + (s_j * d_j) * vj
            new_state = new_state + (kj * math.exp((t_dec - 1.0 - j) * lg)) * vj
        snew_ref[h] = new_state
        o_ref[:, h * RET_V_DIM:(h + 1) * RET_V_DIM] = _rms_norm(o) * _silu(gr)


def _sample_retention(qk, kt, v, gr, state, *, t_dec):
    n_dec, t_pad, _ = qk.shape
    return pl.pallas_call(
        functools.partial(_dec_ret_kernel, t_dec=t_dec, t_pad=t_pad),
        grid=(n_dec,),
        in_specs=[
            pl.BlockSpec((None, t_pad, 2 * RET_HEADS * RET_K_DIM), lambda s: (s, 0, 0)),
            pl.BlockSpec((None, RET_HEADS, RET_K_DIM, t_pad), lambda s: (s, 0, 0, 0)),
            pl.BlockSpec((None, t_pad, RET_HEADS * RET_V_DIM), lambda s: (s, 0, 0)),
            pl.BlockSpec((None, t_pad, RET_HEADS * RET_V_DIM), lambda s: (s, 0, 0)),
            pl.BlockSpec((None, RET_HEADS, RET_K_DIM, RET_V_DIM), lambda s: (s, 0, 0, 0)),
        ],
        out_specs=[
            pl.BlockSpec((None, t_pad, RET_HEADS * RET_V_DIM), lambda s: (s, 0, 0)),
            pl.BlockSpec((None, RET_HEADS, RET_K_DIM, RET_V_DIM), lambda s: (s, 0, 0, 0)),
        ],
        out_shape=[
            jax.ShapeDtypeStruct((n_dec, t_pad, RET_HEADS * RET_V_DIM), F32),
            jax.ShapeDtypeStruct(state.shape, F32),
        ],
        compiler_params=_params("parallel"),
        name="sample_retention",
    )(qk, kt, v, gr, state)


def _sample_attention_inputs(q, k, v, n_dec, t_dec):
    qh = q.reshape(n_dec, t_dec, DA_HEADS, DA_V_DIM) * (DA_HEAD_DIM ** -0.5)
    first = jnp.arange(DA_V_DIM) < DA_HEAD_DIM
    q_rows = jnp.stack([jnp.where(first, qh, 0.0), jnp.where(first, 0.0, qh)], axis=1)
    q_rows = q_rows.reshape(n_dec, 2 * t_dec * DA_HEADS, DA_V_DIM).astype(BF16)
    pad = ((0, 0), (0, PAGE_SIZE - t_dec * DA_HEADS), (0, 0))
    k_new = jnp.pad(k.reshape(n_dec, t_dec * DA_HEADS, DA_V_DIM), pad)
    v_new = jnp.pad(v.reshape(n_dec, t_dec * DA_HEADS, DA_V_DIM), pad)
    return q_rows, k_new, v_new


def kernel(x_prompt, x_sample, cache_k, cache_v, state_ret, page_table, ln_g, ln_b, ffn1_w_gu, ffn1_w_down,
           w_in, da_lambda, da_subln_g, w_branch_a, w_branch_b, w_out, ffn2_w_gu, ffn2_w_down):
    depth = w_in.shape[0]
    bsz, seq, _ = x_prompt.shape
    n_dec, t_dec, _ = x_sample.shape
    alpha = (2.0 * depth) ** 0.25
    t_pad = 8
    tm_p, tm_s = 512, n_dec * t_dec
    tq = 256

    xp = x_prompt.reshape(bsz * seq, D_MODEL)
    xs = x_sample.reshape(n_dec * t_dec, D_MODEL)
    qkr_scale = jnp.concatenate([jnp.ones((RET_HEADS * RET_K_DIM,), F32),
                                 jnp.full((RET_HEADS * RET_K_DIM,), RET_K_DIM ** -0.5, F32)]).reshape(1, D_MODEL)
    outs = [[] for _ in range(6)]
    for l in range(depth):
        lam_init = _lambda_init(l)
        w1_gu, w1_d = ffn1_w_gu[l].astype(BF16), ffn1_w_down[l].astype(BF16)
        w2_gu, w2_d = ffn2_w_gu[l].astype(BF16), ffn2_w_down[l].astype(BF16)
        w_in_l = w_in[l].astype(BF16)
        w_a, w_b, w_o = (w[l].astype(BF16) for w in (w_branch_a, w_branch_b, w_out))
        ln = [(ln_g[l, s].reshape(1, D_MODEL), ln_b[l, s].reshape(1, D_MODEL)) for s in range(3)]
        lp = da_lambda[l].astype(F32)
        subln = da_subln_g[l].reshape(1, DA_V_DIM)

        xp = _ffn_sublayer(xp, w1_gu, w1_d, *ln[0], alpha=alpha, tm=tm_p)
        proj = _input_projection(xp, w_in_l, qkr_scale, act_dtype=BF16, tm=tq, vt_batch=bsz)
        k32, v32, vt = proj[N_GROUPS], proj[N_GROUPS + 1], proj[N_GROUPS + 2]
        as_seq = lambda t: t.reshape(bsz, seq, D_MODEL)
        oa = _prompt_attention(as_seq(proj[G_QA]), as_seq(proj[G_KA]), vt, lp, subln.reshape(DA_V_DIM, 1),
                               lam_init=lam_init, tq=tq)
        ob, s_fin = _prompt_retention(as_seq(proj[G_QKR]), as_seq(proj[G_VR]), as_seq(proj[G_GR]), chunk=256)
        xp = _merge_sublayer(xp, oa.reshape(bsz * seq, D_MODEL), ob.reshape(bsz * seq, D_MODEL),
                             proj[G_GA], proj[G_GB], w_a, w_b, w_o, *ln[1], alpha=alpha, tm=tm_p)
        xp = _ffn_sublayer(xp, w2_gu, w2_d, *ln[2], alpha=alpha, tm=tm_p)
        outs[0].append(k32.reshape(bsz, seq, DA_HEADS, DA_V_DIM))
        outs[1].append(v32.reshape(bsz, seq, DA_HEADS, DA_V_DIM))
        outs[2].append(s_fin)

        xs = _ffn_sublayer(xs, w1_gu, w1_d, *ln[0], alpha=alpha, tm=tm_s)
        proj = _input_projection(xs, w_in_l, qkr_scale, act_dtype=F32, tm=tm_s)
        k32, v32 = proj[N_GROUPS], proj[N_GROUPS + 1]
        q_rows, k_new, v_new = _sample_attention_inputs(proj[G_QA], k32, v32, n_dec, t_dec)
        oa = _sample_attention(page_table, q_rows, cache_k[l], cache_v[l], k_new, v_new, lp, subln,
                               t_dec=t_dec, lam_init=lam_init, n_pp=16)
        tok_pad = lambda t: jnp.pad(t.reshape(n_dec, t_dec, -1), ((0, 0), (0, t_pad - t_dec), (0, 0)))
        qk_r = tok_pad(proj[G_QKR])
        kt_r = qk_r[:, :, RET_HEADS * RET_K_DIM:].reshape(n_dec, t_pad, RET_HEADS, RET_K_DIM)
        kt_r = kt_r.transpose(0, 2, 3, 1)
        ob, s_new = _sample_retention(qk_r, kt_r, tok_pad(proj[G_VR]), tok_pad(proj[G_GR]), state_ret[l],
                                      t_dec=t_dec)
        ob = ob[:, :t_dec].reshape(n_dec * t_dec, D_MODEL)
        xs = _merge_sublayer(xs, oa.reshape(n_dec * t_dec, D_MODEL), ob, proj[G_GA], proj[G_GB],
                             w_a, w_b, w_o, *ln[1], alpha=alpha, tm=tm_s)
        xs = _ffn_sublayer(xs, w2_gu, w2_d, *ln[2], alpha=alpha, tm=tm_s)
        outs[3].append(k32.reshape(n_dec, t_dec, DA_HEADS, DA_V_DIM))
        outs[4].append(v32.reshape(n_dec, t_dec, DA_HEADS, DA_V_DIM))
        outs[5].append(s_new)

    k_p, v_p, s_p, k_s, v_s, s_s = (jnp.stack(o) for o in outs)
    return (xp.reshape(bsz, seq, D_MODEL), xs.reshape(n_dec, t_dec, D_MODEL), k_p, v_p, s_p, k_s, v_s, s_s)
```

```python
import functools
import math

import numpy as np
import jax
import jax.numpy as jnp
from jax import lax
from jax.experimental import pallas as pl
from jax.experimental.pallas import tpu as pltpu

D_MODEL = 1024
PAST_LEN = 8192
PAGE_SIZE = 128
DA_HEADS = 8
DA_HEAD_DIM = 64
DA_V_DIM = 2 * DA_HEAD_DIM
RET_HEADS = 4
RET_K_DIM = 128
RET_V_DIM = 256
D_FF = 2816
LN_EPS = 1e-5
N_GROUPS = 8
G_QA, G_KA, G_VA, G_QKR, G_VR, G_GR, G_GA, G_GB = range(N_GROUPS)

VMEM_LIMIT_BYTES = 48 * 1024 * 1024
NEG_BIG = -1e30
ONES_ROWS = 16
MXU_DIM = 256
FF_CHUNKS = tuple((s, min(3 * MXU_DIM, D_FF - s)) for s in range(0, D_FF, 3 * MXU_DIM))
assert all(size % MXU_DIM == 0 for _, size in FF_CHUNKS)
PROJ_CHUNK = 2 * MXU_DIM

F32 = jnp.float32
BF16 = jnp.bfloat16
_NT = (((1,), (1,)), ((), ()))


def _alibi_slopes():
    return (2.0 ** (-8.0 * np.arange(1, DA_HEADS + 1) / DA_HEADS)).astype(np.float32)


def _retention_log_decay():
    return np.log(1.0 - 2.0 ** (-5.0 - np.arange(RET_HEADS))).astype(np.float32)


def _lambda_init(layer):
    return 0.8 - 0.6 * math.exp(-0.3 * layer)


def _params(*semantics):
    return pltpu.CompilerParams(dimension_semantics=semantics, vmem_limit_bytes=VMEM_LIMIT_BYTES)


def _layer_norm(y, g, b):
    mu = jnp.mean(y, axis=-1, keepdims=True)
    d = y - mu
    var = jnp.mean(d * d, axis=-1, keepdims=True)
    return d * lax.rsqrt(var + LN_EPS) * g + b


def _rms_norm(o):
    return o * lax.rsqrt(jnp.mean(o * o, axis=-1, keepdims=True) + LN_EPS)


def _silu(a):
    return a * jax.nn.sigmoid(a)


def _diff_lambda(lp_ref, lam_init):
    lp = lp_ref[...]
    a = jnp.sum(lp[0:1] * lp[1:2], axis=-1, keepdims=True)
    b = jnp.sum(lp[2:3] * lp[3:4], axis=-1, keepdims=True)
    return jnp.exp(a) - jnp.exp(b) + lam_init


def _ffn_kernel(x_ref, wgu_ref, wd_ref, g_ref, b_ref, o_ref, *, alpha):
    x = x_ref[...]
    xb = x.astype(BF16)
    acc = None
    for start, size in FF_CHUNKS:
        gate = jnp.dot(xb, wgu_ref[:, start:start + size], preferred_element_type=F32)
        up = jnp.dot(xb, wgu_ref[:, D_FF + start:D_FF + start + size], preferred_element_type=F32)
        h = (_silu(gate) * up).astype(BF16)
        part = jnp.dot(h, wd_ref[start:start + size, :], preferred_element_type=F32)
        acc = part if acc is None else acc + part
    o_ref[...] = _layer_norm(alpha * x + 0.5 * acc, g_ref[...], b_ref[...])


def _ffn_sublayer(x, w_gu, w_down, g, b, *, alpha, tm):
    n = x.shape[0]
    resident = functools.partial(pl.BlockSpec, index_map=lambda i: (0, 0), pipeline_mode=pl.Buffered(1))
    return pl.pallas_call(
        functools.partial(_ffn_kernel, alpha=alpha),
        grid=(n // tm,),
        in_specs=[
            pl.BlockSpec((tm, D_MODEL), lambda i: (i, 0)),
            resident((D_MODEL, 2 * D_FF)),
            resident((D_FF, D_MODEL)),
            resident((1, D_MODEL)),
            resident((1, D_MODEL)),
        ],
        out_specs=pl.BlockSpec((tm, D_MODEL), lambda i: (i, 0)),
        out_shape=jax.ShapeDtypeStruct((n, D_MODEL), F32),
        compiler_params=_params("parallel"),
        name="ffn_sublayer",
    )(x, w_gu, w_down, g, b)


def _proj_kernel(x_ref, w_ref, scale_ref, *refs, emit_vt):
    out_refs = refs[:N_GROUPS]
    k32_ref, v32_ref = refs[N_GROUPS:N_GROUPS + 2]
    xb = x_ref[...].astype(BF16)
    for grp in range(N_GROUPS):
        for start in range(0, D_MODEL, PROJ_CHUNK):
            cols = slice(start, start + PROJ_CHUNK)
            y = jnp.dot(xb, w_ref[:, grp * D_MODEL + start:grp * D_MODEL + start + PROJ_CHUNK],
                        preferred_element_type=F32)
            if grp == G_QKR:
                y = y * scale_ref[:, cols]
            out_refs[grp][:, cols] = y.astype(out_refs[grp].dtype)
            if grp == G_KA:
                k32_ref[:, cols] = y
            if grp == G_VA:
                v32_ref[:, cols] = y
                if emit_vt:
                    vt_ref = refs[N_GROUPS + 2]
                    for hh in range(PROJ_CHUNK // DA_V_DIM):
                        head = start // DA_V_DIM + hh
                        vt_ref[head, 0:DA_V_DIM, :] = y[:, hh * DA_V_DIM:(hh + 1) * DA_V_DIM].T.astype(BF16)
                        vt_ref[head, DA_V_DIM:DA_V_DIM + ONES_ROWS, :] = jnp.ones(
                            (ONES_ROWS, y.shape[0]), BF16)


def _input_projection(x, w_in, qkr_scale, *, act_dtype, tm, vt_batch=None):
    n = x.shape[0]
    resident = functools.partial(pl.BlockSpec, index_map=lambda i: (0, 0), pipeline_mode=pl.Buffered(1))
    row_spec = pl.BlockSpec((tm, D_MODEL), lambda i: (i, 0))
    out_specs = [row_spec] * (N_GROUPS + 2)
    out_shape = [jax.ShapeDtypeStruct((n, D_MODEL), act_dtype)] * N_GROUPS + [
        jax.ShapeDtypeStruct((n, D_MODEL), F32)] * 2
    if vt_batch is not None:
        nk = n // vt_batch // tm
        vt_rows = DA_V_DIM + ONES_ROWS
        out_specs.append(pl.BlockSpec((None, DA_HEADS, None, vt_rows, tm), lambda i: (i // nk, 0, i % nk, 0, 0)))
        out_shape.append(jax.ShapeDtypeStruct((vt_batch, DA_HEADS, nk, vt_rows, tm), BF16))
    return pl.pallas_call(
        functools.partial(_proj_kernel, emit_vt=vt_batch is not None),
        grid=(n // tm,),
        in_specs=[row_spec, resident((D_MODEL, N_GROUPS * D_MODEL)), resident((1, D_MODEL))],
        out_specs=out_specs,
        out_shape=out_shape,
        compiler_params=_params("parallel"),
        name="input_projection",
    )(x, w_in, qkr_scale)


HEADS_PER_STEP = 4


def _attn_kernel(slopes_ref, q_ref, k_ref, vt_ref, lp_ref, g_ref, o_ref,
                 qst_ref, kfeat_ref, mask_ref, s_ref, p_ref, alpha_ref, m_ref, acc_ref, *, tq, lam_init):
    hp = pl.program_id(1)
    i = pl.program_id(2)
    heads = range(HEADS_PER_STEP)
    slopes = [slopes_ref[hp * HEADS_PER_STEP + hh] for hh in heads]

    @pl.when(i == 0)
    def _():
        sub = lax.broadcasted_iota(jnp.int32, (DA_V_DIM, 2 * tq), 0)
        qrel = (lax.broadcasted_iota(jnp.int32, (DA_V_DIM, 2 * tq), 1) & (tq - 1)).astype(F32)
        for hh in heads:
            feat = jnp.where(sub == 0, slopes[hh], jnp.where(sub == 1, -slopes[hh] * qrel, 0.0))
            qst_ref[hh, DA_V_DIM:2 * DA_V_DIM, :] = feat.astype(BF16)
        lane = lax.broadcasted_iota(jnp.int32, (tq, DA_V_DIM), 1)
        krel = lax.broadcasted_iota(jnp.int32, (tq, DA_V_DIM), 0).astype(F32)
        kfeat_ref[...] = jnp.where(lane == 0, krel, jnp.where(lane == 1, 1.0, 0.0)).astype(BF16)
        key = lax.broadcasted_iota(jnp.int32, (tq, 2 * tq), 0)
        qry = lax.broadcasted_iota(jnp.int32, (tq, 2 * tq), 1) & (tq - 1)
        mask_ref[...] = jnp.where(key <= qry, 0.0, NEG_BIG)

    for hh in heads:
        q = q_ref[:, hh * DA_V_DIM:(hh + 1) * DA_V_DIM]
        qt = (q.astype(F32) * (DA_HEAD_DIM ** -0.5)).T.astype(BF16)
        sub = lax.broadcasted_iota(jnp.int32, qt.shape, 0)
        zero = jnp.zeros_like(qt)
        qst_ref[hh, 0:DA_V_DIM, 0:tq] = jnp.where(sub < DA_HEAD_DIM, qt, zero)
        qst_ref[hh, 0:DA_V_DIM, tq:2 * tq] = jnp.where(sub >= DA_HEAD_DIM, qt, zero)

    m_ref[...] = jnp.full_like(m_ref, NEG_BIG)
    acc_ref[...] = jnp.zeros_like(acc_ref)

    def block_of(r):
        if isinstance(r, int):
            return i if r == 0 else r - 1
        return jnp.where(r == 0, i, r - 1)

    def qk(r, slot):
        j = block_of(r)
        for hh in heads:
            k = k_ref[pl.ds(j * tq, tq), hh * DA_V_DIM:(hh + 1) * DA_V_DIM]
            k_aug = jnp.concatenate([k, kfeat_ref[...]], axis=1)
            s = jnp.dot(k_aug, qst_ref[hh], preferred_element_type=F32)
            if isinstance(r, int) and r == 0:
                s = s + mask_ref[...]
            s_ref[slot, hh] = s

    def softmax(r, slot):
        j = block_of(r)
        for hh in heads:
            s = s_ref[slot, hh]
            c = slopes[hh] * ((j - i) * tq).astype(F32)
            m_prev = m_ref[hh]
            m_new = jnp.maximum(m_prev, jnp.max(s, axis=0, keepdims=True) + c)
            alpha_ref[slot, hh] = jnp.exp(m_prev - m_new)
            p_ref[slot, hh] = jnp.exp(s - (m_new - c)).astype(BF16)
            m_ref[hh] = m_new

    def pv(r, slot):
        j = block_of(r)
        for hh in heads:
            acc_ref[hh] = alpha_ref[slot, hh] * acc_ref[hh] + jnp.dot(
                vt_ref[hh, j], p_ref[slot, hh], preferred_element_type=F32)

    def tick(u, parity, *, with_pv=True, with_qk=True):
        if with_pv:
            pv(u - 2, parity)
        softmax(u - 1, 1 - parity)
        if with_qk:
            qk(u, parity)

    def two_ticks(t, carry):
        u = 2 * t
        qk(u, 0)
        softmax(u - 1, 1)
        pv(u - 2, 0)
        qk(u + 1, 1)
        softmax(u, 0)
        pv(u - 1, 1)
        return carry

    n = i + 1
    qk(0, 0)

    @pl.when(n >= 2)
    def _():
        tick(1, 1, with_pv=False)

    lax.fori_loop(1, jnp.maximum(n - 2, 0) // 2 + 1, two_ticks, 0)

    @pl.when((n >= 2) & (n % 2 == 0))
    def _():
        tick(n, 0, with_qk=False)
        pv(n - 1, 1)

    @pl.when((n >= 3) & (n % 2 == 1))
    def _():
        tick(n - 1, 0)
        tick(n, 1, with_qk=False)
        pv(n - 1, 0)

    @pl.when(n == 1)
    def _():
        softmax(0, 0)
        pv(0, 0)

    lam = _diff_lambda(lp_ref, lam_init)
    for hh in heads:
        o = acc_ref[hh, 0:DA_V_DIM, :] / acc_ref[hh, DA_V_DIM:DA_V_DIM + 1, :]
        d = o[:, 0:tq] - lam * o[:, tq:2 * tq]
        r = d * lax.rsqrt(jnp.mean(d * d, axis=0, keepdims=True) + LN_EPS) * g_ref[...] * (1.0 - lam_init)
        o_ref[:, hh * DA_V_DIM:(hh + 1) * DA_V_DIM] = r.T.astype(o_ref.dtype)


def _prompt_attention(q, k, vt, lp, subln_g_col, *, lam_init, tq):
    bsz, seq, _ = q.shape
    assert tq & (tq - 1) == 0 and tq <= 256 and seq % tq == 0 and DA_HEADS % HEADS_PER_STEP == 0
    nk = seq // tq
    vt_rows = DA_V_DIM + ONES_ROWS
    hw = HEADS_PER_STEP * DA_V_DIM
    return pl.pallas_call(
        functools.partial(_attn_kernel, tq=tq, lam_init=lam_init),
        grid=(bsz, DA_HEADS // HEADS_PER_STEP, nk),
        in_specs=[
            pl.BlockSpec(memory_space=pltpu.SMEM),
            pl.BlockSpec((None, tq, hw), lambda b, h, i: (b, i, h)),
            pl.BlockSpec((None, seq, hw), lambda b, h, i: (b, 0, h)),
            pl.BlockSpec((None, HEADS_PER_STEP, nk, vt_rows, tq), lambda b, h, i: (b, h, 0, 0, 0)),
            pl.BlockSpec((4, DA_HEAD_DIM), lambda b, h, i: (0, 0)),
            pl.BlockSpec((DA_V_DIM, 1), lambda b, h, i: (0, 0)),
        ],
        out_specs=pl.BlockSpec((None, tq, hw), lambda b, h, i: (b, i, h)),
        out_shape=jax.ShapeDtypeStruct((bsz, seq, DA_HEADS * DA_V_DIM), BF16),
        scratch_shapes=[
            pltpu.VMEM((HEADS_PER_STEP, 2 * DA_V_DIM, 2 * tq), BF16),
            pltpu.VMEM((tq, DA_V_DIM), BF16),
            pltpu.VMEM((tq, 2 * tq), F32),
            pltpu.VMEM((2, HEADS_PER_STEP, tq, 2 * tq), F32),
            pltpu.VMEM((2, HEADS_PER_STEP, tq, 2 * tq), BF16),
            pltpu.VMEM((2, HEADS_PER_STEP, 1, 2 * tq), F32),
            pltpu.VMEM((HEADS_PER_STEP, 1, 2 * tq), F32),
            pltpu.VMEM((HEADS_PER_STEP, vt_rows, 2 * tq), F32),
        ],
        compiler_params=_params("parallel", "parallel", "arbitrary"),
        name="prompt_diff_attention",
    )(jnp.asarray(_alibi_slopes()), q, k, vt, lp, subln_g_col)


def _ret_kernel(qk_ref, v_ref, gr_ref, o_ref, sfin_ref, state_ref, decay_ref, cross_ref, kdec_ref, *, chunk):
    c = pl.program_id(1)
    log_g = _retention_log_decay()

    @pl.when(c == 0)
    def _():
        state_ref[...] = jnp.zeros_like(state_ref)
        row = lax.broadcasted_iota(jnp.int32, (chunk, chunk), 0)
        col = lax.broadcasted_iota(jnp.int32, (chunk, chunk), 1)
        diff = (row - col).astype(F32)
        idx_v = lax.broadcasted_iota(jnp.int32, (chunk, RET_V_DIM), 0).astype(F32)
        idx_k = lax.broadcasted_iota(jnp.int32, (chunk, RET_K_DIM), 0).astype(F32)
        for h in range(RET_HEADS):
            lg = float(log_g[h])
            decay_ref[h] = jnp.where(diff >= 0, jnp.exp(jnp.maximum(diff, 0.0) * lg), 0.0)
            cross_ref[h] = jnp.exp((idx_v + 1.0) * lg)
            kdec_ref[h] = jnp.exp((chunk - 1.0 - idx_k) * lg)

    for h in range(RET_HEADS):
        q = qk_ref[:, h * RET_K_DIM:(h + 1) * RET_K_DIM]
        k = qk_ref[:, (RET_HEADS + h) * RET_K_DIM:(RET_HEADS + h + 1) * RET_K_DIM]
        v = v_ref[:, h * RET_V_DIM:(h + 1) * RET_V_DIM]
        gr = gr_ref[:, h * RET_V_DIM:(h + 1) * RET_V_DIM]
        scores = lax.dot_general(q, k, _NT, preferred_element_type=F32) * decay_ref[h]
        state = state_ref[h]
        o = (jnp.dot(scores.astype(BF16), v, preferred_element_type=F32)
             + jnp.dot(q, state.astype(BF16), preferred_element_type=F32) * cross_ref[h])
        k_w = (k.astype(F32) * kdec_ref[h]).astype(BF16)
        new_state = math.exp(chunk * float(log_g[h])) * state + lax.dot_general(
            k_w, v, (((0,), (0,)), ((), ())), preferred_element_type=F32)
        state_ref[h] = new_state
        o_ref[:, h * RET_V_DIM:(h + 1) * RET_V_DIM] = (_rms_norm(o) * _silu(gr.astype(F32))).astype(o_ref.dtype)

    @pl.when(c == pl.num_programs(1) - 1)
    def _():
        sfin_ref[...] = state_ref[...]


def _prompt_retention(qk, v, gr, *, chunk):
    bsz, seq, _ = qk.shape
    row_spec = pl.BlockSpec((None, chunk, D_MODEL), lambda b, c: (b, c, 0))
    state_shape = (RET_HEADS, RET_K_DIM, RET_V_DIM)
    return pl.pallas_call(
        functools.partial(_ret_kernel, chunk=chunk),
        grid=(bsz, seq // chunk),
        in_specs=[row_spec, row_spec, row_spec],
        out_specs=[row_spec, pl.BlockSpec((None,) + state_shape, lambda b, c: (b, 0, 0, 0))],
        out_shape=[
            jax.ShapeDtypeStruct((bsz, seq, RET_HEADS * RET_V_DIM), BF16),
            jax.ShapeDtypeStruct((bsz,) + state_shape, F32),
        ],
        scratch_shapes=[
            pltpu.VMEM(state_shape, F32),
            pltpu.VMEM((RET_HEADS, chunk, chunk), F32),
            pltpu.VMEM((RET_HEADS, chunk, RET_V_DIM), F32),
            pltpu.VMEM((RET_HEADS, chunk, RET_K_DIM), F32),
        ],
        compiler_params=_params("parallel", "arbitrary"),
        name="prompt_retention",
    )(qk, v, gr)


def _merge_kernel(x_ref, oa_ref, ob_ref, ga_ref, gb_ref, wa_ref, wb_ref, wo_ref, g_ref, b_ref, o_ref, *, alpha):
    a = jnp.dot(oa_ref[...].astype(BF16), wa_ref[...], preferred_element_type=F32)
    b = jnp.dot(ob_ref[...].astype(BF16), wb_ref[...], preferred_element_type=F32)
    m = jax.nn.sigmoid(ga_ref[...].astype(F32)) * a + jax.nn.sigmoid(gb_ref[...].astype(F32)) * b
    y = alpha * x_ref[...] + jnp.dot(m.astype(BF16), wo_ref[...], preferred_element_type=F32)
    o_ref[...] = _layer_norm(y, g_ref[...], b_ref[...])


def _merge_sublayer(x, oa, ob, ga, gb, w_a, w_b, w_o, g, b, *, alpha, tm):
    n = x.shape[0]
    row_spec = pl.BlockSpec((tm, D_MODEL), lambda i: (i, 0))
    w_spec = pl.BlockSpec((D_MODEL, D_MODEL), lambda i: (0, 0))
    vec_spec = pl.BlockSpec((1, D_MODEL), lambda i: (0, 0))
    return pl.pallas_call(
        functools.partial(_merge_kernel, alpha=alpha),
        grid=(n // tm,),
        in_specs=[row_spec] * 5 + [w_spec] * 3 + [vec_spec] * 2,
        out_specs=row_spec,
        out_shape=jax.ShapeDtypeStruct((n, D_MODEL), F32),
        compiler_params=_params("parallel"),
        name="merge_sublayer",
    )(x, oa, ob, ga, gb, w_a, w_b, w_o, g, b)


def _dec_attn_kernel(pt_ref, q_ref, slope_ref, *refs, t_dec, lam_init, n_pp):
    del pt_ref
    k_refs, v_refs = refs[:n_pp], refs[n_pp:2 * n_pp]
    kn_ref, vn_ref, lp_ref, g_ref, o_ref, mb_ref, mbn_ref, m_ref, l_ref, acc_ref = refs[2 * n_pp:]
    p = pl.program_id(1)
    n_rows = 2 * t_dec * DA_HEADS
    page_rows = PAGE_SIZE * DA_HEADS

    @pl.when(p == 0)
    def _():
        m_ref[...] = jnp.full_like(m_ref, NEG_BIG)
        l_ref[...] = jnp.zeros_like(l_ref)
        acc_ref[...] = jnp.zeros_like(acc_ref)
        row = lax.broadcasted_iota(jnp.int32, (n_rows, page_rows), 0)
        col = lax.broadcasted_iota(jnp.int32, (n_rows, page_rows), 1)
        same_head = ((col - row) & (DA_HEADS - 1)) == 0
        tok = (row // DA_HEADS) % t_dec
        rel = (col // DA_HEADS - tok - PAST_LEN).astype(F32)
        mb_ref[...] = jnp.where(same_head, slope_ref[...] * rel, NEG_BIG)
        rown = lax.broadcasted_iota(jnp.int32, (n_rows, PAGE_SIZE), 0)
        coln = lax.broadcasted_iota(jnp.int32, (n_rows, PAGE_SIZE), 1)
        tokq = (rown // DA_HEADS) % t_dec
        tokk = coln // DA_HEADS
        ok = (((coln - rown) & (DA_HEADS - 1)) == 0) & (tokk <= tokq)
        mbn_ref[...] = jnp.where(ok, slope_ref[...] * (tokk - tokq).astype(F32), NEG_BIG)

    def update(blocks):
        q = q_ref[...]
        scores = [lax.dot_general(q, kf.astype(BF16), _NT, preferred_element_type=F32) + mb
                  for kf, _, mb in blocks]
        m_prev = m_ref[...]
        m_new = m_prev
        for s in scores:
            m_new = jnp.maximum(m_new, jnp.max(s, axis=-1, keepdims=True))
        alpha = jnp.exp(m_prev - m_new)
        l_new = alpha * l_ref[...]
        acc_new = alpha * acc_ref[...]
        for s, (_, vf, _) in zip(scores, blocks):
            pr = jnp.exp(s - m_new)
            l_new = l_new + jnp.sum(pr, axis=-1, keepdims=True)
            acc_new = acc_new + jnp.dot(pr.astype(BF16), vf.astype(BF16), preferred_element_type=F32)
        l_ref[...] = l_new
        acc_ref[...] = acc_new
        m_ref[...] = m_new

    slope = slope_ref[...]
    update([(k_refs[r][...].reshape(page_rows, DA_V_DIM), v_refs[r][...].reshape(page_rows, DA_V_DIM),
             mb_ref[...] + slope * ((p * n_pp + r) * PAGE_SIZE).astype(F32)) for r in range(n_pp)])

    @pl.when(p == pl.num_programs(1) - 1)
    def _():
        update([(kn_ref[...], vn_ref[...], mbn_ref[...])])
        o = acc_ref[...] / l_ref[...]
        lam = _diff_lambda(lp_ref, lam_init)
        half = n_rows // 2
        d = o[0:half] - lam * o[half:n_rows]
        o_ref[...] = _rms_norm(d) * g_ref[...] * (1.0 - lam_init)


def _sample_attention(page_table, q_rows, cache_k, cache_v, k_new, v_new, lp, subln_g, *,
                      t_dec, lam_init, n_pp):
    n_dec, n_pages = page_table.shape
    n_rows = 2 * t_dec * DA_HEADS
    assert t_dec * DA_HEADS <= PAGE_SIZE and n_pages % n_pp == 0
    slope_rows = jnp.asarray(np.tile(_alibi_slopes(), 2 * t_dec).reshape(n_rows, 1))
    page_specs = [pl.BlockSpec((None, PAGE_SIZE, DA_HEADS, DA_V_DIM),
                               functools.partial(lambda s, p, pt, r: (pt[s, p * n_pp + r], 0, 0, 0), r=r))
                  for r in range(n_pp)]
    new_spec = pl.BlockSpec((None, PAGE_SIZE, DA_V_DIM), lambda s, p, pt: (s, 0, 0))
    return pl.pallas_call(
        functools.partial(_dec_attn_kernel, t_dec=t_dec, lam_init=lam_init, n_pp=n_pp),
        grid_spec=pltpu.PrefetchScalarGridSpec(
            num_scalar_prefetch=1,
            grid=(n_dec, n_pages // n_pp),
            in_specs=[
                pl.BlockSpec((None, n_rows, DA_V_DIM), lambda s, p, pt: (s, 0, 0)),
                pl.BlockSpec((n_rows, 1), lambda s, p, pt: (0, 0)),
                *page_specs, *page_specs, new_spec, new_spec,
                pl.BlockSpec((4, DA_HEAD_DIM), lambda s, p, pt: (0, 0)),
                pl.BlockSpec((1, DA_V_DIM), lambda s, p, pt: (0, 0)),
            ],
            out_specs=pl.BlockSpec((None, n_rows // 2, DA_V_DIM), lambda s, p, pt: (s, 0, 0)),
            scratch_shapes=[
                pltpu.VMEM((n_rows, PAGE_SIZE * DA_HEADS), F32),
                pltpu.VMEM((n_rows, PAGE_SIZE), F32),
                pltpu.VMEM((n_rows, 1), F32),
                pltpu.VMEM((n_rows, 1), F32),
                pltpu.VMEM((n_rows, DA_V_DIM), F32),
            ],
        ),
        out_shape=jax.ShapeDtypeStruct((n_dec, n_rows // 2, DA_V_DIM), F32),
        compiler_params=_params("parallel", "arbitrary"),
        name="sample_diff_attention",
    )(page_table, q_rows, slope_rows, *([cache_k] * n_pp), *([cache_v] * n_pp), k_new, v_new, lp, subln_g)


def _dec_ret_kernel(qk_ref, kt_ref, v_ref, gr_ref, st_ref, o_ref, snew_ref, *, t_dec, t_pad):
    log_g = _retention_log_decay()
    qk_all = qk_ref[...]
    v_all = v_ref[...]
    gr_all = gr_ref[...]
    tok = lax.broadcasted_iota(jnp.int32, (t_pad, 1), 0).astype(F32)
    for h in range(RET_HEADS):
        lg = float(log_g[h])
        q = qk_all[:, h * RET_K_DIM:(h + 1) * RET_K_DIM]
        k = qk_all[:, (RET_HEADS + h) * RET_K_DIM:(RET_HEADS + h + 1) * RET_K_DIM]
        v = v_all[:, h * RET_V_DIM:(h + 1) * RET_V_DIM]
        gr = gr_all[:, h * RET_V_DIM:(h + 1) * RET_V_DIM]
        kt = kt_ref[h]
        state = st_ref[h]
        o = jnp.dot(q.astype(BF16), state.astype(BF16), preferred_element_type=F32) * jnp.exp((tok + 1.0) * lg)
        new_state = math.exp(t_dec * lg) * state
        for j in range(t_dec):
            kj = kt[:, j:j + 1]
            vj = v[j:j + 1, :]
            s_j = jnp.sum(q * k[j:j + 1, :], axis=-1, keepdims=True)
            d_j = jnp.where(tok >= j, jnp.exp(jnp.maximum(tok - j, 0.0) * lg), 0.0)
            o = o + (s_j * d_j) * vj
            new_state = new_state + (kj * math.exp((t_dec - 1.0 - j) * lg)) * vj
        snew_ref[h] = new_state
        o_ref[:, h * RET_V_DIM:(h + 1) * RET_V_DIM] = _rms_norm(o) * _silu(gr)


def _sample_retention(qk, kt, v, gr, state, *, t_dec):
    n_dec, t_pad, _ = qk.shape
    return pl.pallas_call(
        functools.partial(_dec_ret_kernel, t_dec=t_dec, t_pad=t_pad),
        grid=(n_dec,),
        in_specs=[
            pl.BlockSpec((None, t_pad, 2 * RET_HEADS * RET_K_DIM), lambda s: (s, 0, 0)),
            pl.BlockSpec((None, RET_HEADS, RET_K_DIM, t_pad), lambda s: (s, 0, 0, 0)),
            pl.BlockSpec((None, t_pad, RET_HEADS * RET_V_DIM), lambda s: (s, 0, 0)),
            pl.BlockSpec((None, t_pad, RET_HEADS * RET_V_DIM), lambda s: (s, 0, 0)),
            pl.BlockSpec((None, RET_HEADS, RET_K_DIM, RET_V_DIM), lambda s: (s, 0, 0, 0)),
        ],
        out_specs=[
            pl.BlockSpec((None, t_pad, RET_HEADS * RET_V_DIM), lambda s: (s, 0, 0)),
            pl.BlockSpec((None, RET_HEADS, RET_K_DIM, RET_V_DIM), lambda s: (s, 0, 0, 0)),
        ],
        out_shape=[
            jax.ShapeDtypeStruct((n_dec, t_pad, RET_HEADS * RET_V_DIM), F32),
            jax.ShapeDtypeStruct(state.shape, F32),
        ],
        compiler_params=_params("parallel"),
        name="sample_retention",
    )(qk, kt, v, gr, state)


def _sample_attention_inputs(q, k, v, n_dec, t_dec):
    qh = q.reshape(n_dec, t_dec, DA_HEADS, DA_V_DIM) * (DA_HEAD_DIM ** -0.5)
    first = jnp.arange(DA_V_DIM) < DA_HEAD_DIM
    q_rows = jnp.stack([jnp.where(first, qh, 0.0), jnp.where(first, 0.0, qh)], axis=1)
    q_rows = q_rows.reshape(n_dec, 2 * t_dec * DA_HEADS, DA_V_DIM).astype(BF16)
    pad = ((0, 0), (0, PAGE_SIZE - t_dec * DA_HEADS), (0, 0))
    k_new = jnp.pad(k.reshape(n_dec, t_dec * DA_HEADS, DA_V_DIM), pad)
    v_new = jnp.pad(v.reshape(n_dec, t_dec * DA_HEADS, DA_V_DIM), pad)
    return q_rows, k_new, v_new


def kernel(x_prompt, x_sample, cache_k, cache_v, state_ret, page_table, ln_g, ln_b, ffn1_w_gu, ffn1_w_down,
           w_in, da_lambda, da_subln_g, w_branch_a, w_branch_b, w_out, ffn2_w_gu, ffn2_w_down):
    depth = w_in.shape[0]
    bsz, seq, _ = x_prompt.shape
    n_dec, t_dec, _ = x_sample.shape
    alpha = (2.0 * depth) ** 0.25
    t_pad = 8
    tm_p, tm_s = 512, n_dec * t_dec
    tq = 256

    xp = x_prompt.reshape(bsz * seq, D_MODEL)
    xs = x_sample.reshape(n_dec * t_dec, D_MODEL)
    qkr_scale = jnp.concatenate([jnp.ones((RET_HEADS * RET_K_DIM,), F32),
                                 jnp.full((RET_HEADS * RET_K_DIM,), RET_K_DIM ** -0.5, F32)]).reshape(1, D_MODEL)
    outs = [[] for _ in range(6)]
    for l in range(depth):
        lam_init = _lambda_init(l)
        w1_gu, w1_d = ffn1_w_gu[l].astype(BF16), ffn1_w_down[l].astype(BF16)
        w2_gu, w2_d = ffn2_w_gu[l].astype(BF16), ffn2_w_down[l].astype(BF16)
        w_in_l = w_in[l].astype(BF16)
        w_a, w_b, w_o = (w[l].astype(BF16) for w in (w_branch_a, w_branch_b, w_out))
        ln = [(ln_g[l, s].reshape(1, D_MODEL), ln_b[l, s].reshape(1, D_MODEL)) for s in range(3)]
        lp = da_lambda[l].astype(F32)
        subln = da_subln_g[l].reshape(1, DA_V_DIM)

        xp = _ffn_sublayer(xp, w1_gu, w1_d, *ln[0], alpha=alpha, tm=tm_p)
        proj = _input_projection(xp, w_in_l, qkr_scale, act_dtype=BF16, tm=tq, vt_batch=bsz)
        k32, v32, vt = proj[N_GROUPS], proj[N_GROUPS + 1], proj[N_GROUPS + 2]
        as_seq = lambda t: t.reshape(bsz, seq, D_MODEL)
        oa = _prompt_attention(as_seq(proj[G_QA]), as_seq(proj[G_KA]), vt, lp, subln.reshape(DA_V_DIM, 1),
                               lam_init=lam_init, tq=tq)
        ob, s_fin = _prompt_retention(as_seq(proj[G_QKR]), as_seq(proj[G_VR]), as_seq(proj[G_GR]), chunk=256)
        xp = _merge_sublayer(xp, oa.reshape(bsz * seq, D_MODEL), ob.reshape(bsz * seq, D_MODEL),
                             proj[G_GA], proj[G_GB], w_a, w_b, w_o, *ln[1], alpha=alpha, tm=tm_p)
        xp = _ffn_sublayer(xp, w2_gu, w2_d, *ln[2], alpha=alpha, tm=tm_p)
        outs[0].append(k32.reshape(bsz, seq, DA_HEADS, DA_V_DIM))
        outs[1].append(v32.reshape(bsz, seq, DA_HEADS, DA_V_DIM))
        outs[2].append(s_fin)

        xs = _ffn_sublayer(xs, w1_gu, w1_d, *ln[0], alpha=alpha, tm=tm_s)
        proj = _input_projection(xs, w_in_l, qkr_scale, act_dtype=F32, tm=tm_s)
        k32, v32 = proj[N_GROUPS], proj[N_GROUPS + 1]
        q_rows, k_new, v_new = _sample_attention_inputs(proj[G_QA], k32, v32, n_dec, t_dec)
        oa = _sample_attention(page_table, q_rows, cache_k[l], cache_v[l], k_new, v_new, lp, subln,
                               t_dec=t_dec, lam_init=lam_init, n_pp=16)
        tok_pad = lambda t: jnp.pad(t.reshape(n_dec, t_dec, -1), ((0, 0), (0, t_pad - t_dec), (0, 0)))
        qk_r = tok_pad(proj[G_QKR])
        kt_r = qk_r[:, :, RET_HEADS * RET_K_DIM:].reshape(n_dec, t_pad, RET_HEADS, RET_K_DIM)
        kt_r = kt_r.transpose(0, 2, 3, 1)
        ob, s_new = _sample_retention(qk_r, kt_r, tok_pad(proj[G_VR]), tok_pad(proj[G_GR]), state_ret[l],
                                      t_dec=t_dec)
        ob = ob[:, :t_dec].reshape(n_dec * t_dec, D_MODEL)
        xs = _merge_sublayer(xs, oa.reshape(n_dec * t_dec, D_MODEL), ob, proj[G_GA], proj[G_GB],
                             w_a, w_b, w_o, *ln[1], alpha=alpha, tm=tm_s)
        xs = _ffn_sublayer(xs, w2_gu, w2_d, *ln[2], alpha=alpha, tm=tm_s)
        outs[3].append(k32.reshape(n_dec, t_dec, DA_HEADS, DA_V_DIM))
        outs[4].append(v32.reshape(n_dec, t_dec, DA_HEADS, DA_V_DIM))
        outs[5].append(s_new)

    k_p, v_p, s_p, k_s, v_s, s_s = (jnp.stack(o) for o in outs)
    return (xp.reshape(bsz, seq, D_MODEL), xs.reshape(n_dec, t_dec, D_MODEL), k_p, v_p, s_p, k_s, v_s, s_s)
```

```python
import functools
import math

import numpy as np
import jax
import jax.numpy as jnp
from jax import lax
from jax.experimental import pallas as pl
from jax.experimental.pallas import tpu as pltpu

D_MODEL = 1024
PAST_LEN = 8192
PAGE_SIZE = 128
DA_HEADS = 8
DA_HEAD_DIM = 64
DA_V_DIM = 2 * DA_HEAD_DIM
RET_HEADS = 4
RET_K_DIM = 128
RET_V_DIM = 256
D_FF = 2816
LN_EPS = 1e-5
N_GROUPS = 8
G_QA, G_KA, G_VA, G_QKR, G_VR, G_GR, G_GA, G_GB = range(N_GROUPS)

VMEM_LIMIT_BYTES = 48 * 1024 * 1024
NEG_BIG = -1e30
ONES_ROWS = 16
MXU_DIM = 256
FF_CHUNKS = tuple((s, min(3 * MXU_DIM, D_FF - s)) for s in range(0, D_FF, 3 * MXU_DIM))
assert all(size % MXU_DIM == 0 for _, size in FF_CHUNKS)
PROJ_CHUNK = 2 * MXU_DIM

F32 = jnp.float32
BF16 = jnp.bfloat16
_NT = (((1,), (1,)), ((), ()))


def _alibi_slopes():
    return (2.0 ** (-8.0 * np.arange(1, DA_HEADS + 1) / DA_HEADS)).astype(np.float32)


def _retention_log_decay():
    return np.log(1.0 - 2.0 ** (-5.0 - np.arange(RET_HEADS))).astype(np.float32)


def _lambda_init(layer):
    return 0.8 - 0.6 * math.exp(-0.3 * layer)


def _params(*semantics):
    return pltpu.CompilerParams(dimension_semantics=semantics, vmem_limit_bytes=VMEM_LIMIT_BYTES)


def _layer_norm(y, g, b):
    mu = jnp.mean(y, axis=-1, keepdims=True)
    d = y - mu
    var = jnp.mean(d * d, axis=-1, keepdims=True)
    return d * lax.rsqrt(var + LN_EPS) * g + b


def _rms_norm(o):
    return o * lax.rsqrt(jnp.mean(o * o, axis=-1, keepdims=True) + LN_EPS)


def _silu(a):
    return a * jax.nn.sigmoid(a)


def _diff_lambda(lp_ref, lam_init):
    lp = lp_ref[...]
    a = jnp.sum(lp[0:1] * lp[1:2], axis=-1, keepdims=True)
    b = jnp.sum(lp[2:3] * lp[3:4], axis=-1, keepdims=True)
    return jnp.exp(a) - jnp.exp(b) + lam_init


def _ffn_kernel(x_ref, wgu_ref, wd_ref, g_ref, b_ref, o_ref, *, alpha):
    x = x_ref[...]
    xb = x.astype(BF16)
    acc = None
    for start, size in FF_CHUNKS:
        gate = jnp.dot(xb, wgu_ref[:, start:start + size], preferred_element_type=F32)
        up = jnp.dot(xb, wgu_ref[:, D_FF + start:D_FF + start + size], preferred_element_type=F32)
        h = (_silu(gate) * up).astype(BF16)
        part = jnp.dot(h, wd_ref[start:start + size, :], preferred_element_type=F32)
        acc = part if acc is None else acc + part
    o_ref[...] = _layer_norm(alpha * x + 0.5 * acc, g_ref[...], b_ref[...])


def _ffn_sublayer(x, w_gu, w_down, g, b, *, alpha, tm):
    n = x.shape[0]
    resident = functools.partial(pl.BlockSpec, index_map=lambda i: (0, 0), pipeline_mode=pl.Buffered(1))
    return pl.pallas_call(
        functools.partial(_ffn_kernel, alpha=alpha),
        grid=(n // tm,),
        in_specs=[
            pl.BlockSpec((tm, D_MODEL), lambda i: (i, 0)),
            resident((D_MODEL, 2 * D_FF)),
            resident((D_FF, D_MODEL)),
            resident((1, D_MODEL)),
            resident((1, D_MODEL)),
        ],
        out_specs=pl.BlockSpec((tm, D_MODEL), lambda i: (i, 0)),
        out_shape=jax.ShapeDtypeStruct((n, D_MODEL), F32),
        compiler_params=_params("parallel"),
        name="ffn_sublayer",
    )(x, w_gu, w_down, g, b)


def _proj_kernel(x_ref, w_ref, scale_ref, *refs, emit_vt):
    out_refs = refs[:N_GROUPS]
    k32_ref, v32_ref = refs[N_GROUPS:N_GROUPS + 2]
    xb = x_ref[...].astype(BF16)
    for grp in range(N_GROUPS):
        for start in range(0, D_MODEL, PROJ_CHUNK):
            cols = slice(start, start + PROJ_CHUNK)
            y = jnp.dot(xb, w_ref[:, grp * D_MODEL + start:grp * D_MODEL + start + PROJ_CHUNK],
                        preferred_element_type=F32)
            if grp == G_QKR:
                y = y * scale_ref[:, cols]
            out_refs[grp][:, cols] = y.astype(out_refs[grp].dtype)
            if grp == G_KA:
                k32_ref[:, cols] = y
            if grp == G_VA:
                v32_ref[:, cols] = y
                if emit_vt:
                    vt_ref = refs[N_GROUPS + 2]
                    for hh in range(PROJ_CHUNK // DA_V_DIM):
                        head = start // DA_V_DIM + hh
                        vt_ref[head, 0:DA_V_DIM, :] = y[:, hh * DA_V_DIM:(hh + 1) * DA_V_DIM].T.astype(BF16)
                        vt_ref[head, DA_V_DIM:DA_V_DIM + ONES_ROWS, :] = jnp.ones(
                            (ONES_ROWS, y.shape[0]), BF16)


def _input_projection(x, w_in, qkr_scale, *, act_dtype, tm, vt_batch=None):
    n = x.shape[0]
    resident = functools.partial(pl.BlockSpec, index_map=lambda i: (0, 0), pipeline_mode=pl.Buffered(1))
    row_spec = pl.BlockSpec((tm, D_MODEL), lambda i: (i, 0))
    out_specs = [row_spec] * (N_GROUPS + 2)
    out_shape = [jax.ShapeDtypeStruct((n, D_MODEL), act_dtype)] * N_GROUPS + [
        jax.ShapeDtypeStruct((n, D_MODEL), F32)] * 2
    if vt_batch is not None:
        nk = n // vt_batch // tm
        vt_rows = DA_V_DIM + ONES_ROWS
        out_specs.append(pl.BlockSpec((None, DA_HEADS, None, vt_rows, tm), lambda i: (i // nk, 0, i % nk, 0, 0)))
        out_shape.append(jax.ShapeDtypeStruct((vt_batch, DA_HEADS, nk, vt_rows, tm), BF16))
    return pl.pallas_call(
        functools.partial(_proj_kernel, emit_vt=vt_batch is not None),
        grid=(n // tm,),
        in_specs=[row_spec, resident((D_MODEL, N_GROUPS * D_MODEL)), resident((1, D_MODEL))],
        out_specs=out_specs,
        out_shape=out_shape,
        compiler_params=_params("parallel"),
        name="input_projection",
    )(x, w_in, qkr_scale)


HEADS_PER_STEP = 4


def _attn_kernel(slopes_ref, q_ref, k_ref, vt_ref, lp_ref, g_ref, o_ref,
                 qst_ref, kfeat_ref, mask_ref, s_ref, smax_ref, p_ref, alpha_ref, m_ref, acc_ref, *, tq, lam_init):
    hp = pl.program_id(1)
    i = pl.program_id(2)
    heads = range(HEADS_PER_STEP)
    slopes = [slopes_ref[hp * HEADS_PER_STEP + hh] for hh in heads]

    @pl.when(i == 0)
    def _():
        sub = lax.broadcasted_iota(jnp.int32, (DA_V_DIM, 2 * tq), 0)
        qrel = (lax.broadcasted_iota(jnp.int32, (DA_V_DIM, 2 * tq), 1) & (tq - 1)).astype(F32)
        for hh in heads:
            feat = jnp.where(sub == 0, slopes[hh], jnp.where(sub == 1, -slopes[hh] * qrel, 0.0))
            qst_ref[hh, DA_V_DIM:2 * DA_V_DIM, :] = feat.astype(BF16)
        lane = lax.broadcasted_iota(jnp.int32, (tq, DA_V_DIM), 1)
        krel = lax.broadcasted_iota(jnp.int32, (tq, DA_V_DIM), 0).astype(F32)
        kfeat_ref[...] = jnp.where(lane == 0, krel, jnp.where(lane == 1, 1.0, 0.0)).astype(BF16)
        key = lax.broadcasted_iota(jnp.int32, (tq, 2 * tq), 0)
        qry = lax.broadcasted_iota(jnp.int32, (tq, 2 * tq), 1) & (tq - 1)
        mask_ref[...] = jnp.where(key <= qry, 0.0, NEG_BIG)

    for hh in heads:
        q = q_ref[:, hh * DA_V_DIM:(hh + 1) * DA_V_DIM]
        qt = (q.astype(F32) * (DA_HEAD_DIM ** -0.5)).T.astype(BF16)
        sub = lax.broadcasted_iota(jnp.int32, qt.shape, 0)
        zero = jnp.zeros_like(qt)
        qst_ref[hh, 0:DA_V_DIM, 0:tq] = jnp.where(sub < DA_HEAD_DIM, qt, zero)
        qst_ref[hh, 0:DA_V_DIM, tq:2 * tq] = jnp.where(sub >= DA_HEAD_DIM, qt, zero)

    m_ref[...] = jnp.full_like(m_ref, NEG_BIG)
    acc_ref[...] = jnp.zeros_like(acc_ref)

    def block_of(r):
        if isinstance(r, int):
            return i if r == 0 else r - 1
        return jnp.where(r == 0, i, r - 1)

    def qk(r, slot):
        j = block_of(r)
        for hh in heads:
            k = k_ref[pl.ds(j * tq, tq), hh * DA_V_DIM:(hh + 1) * DA_V_DIM]
            k_aug = jnp.concatenate([k, kfeat_ref[...]], axis=1)
            s = jnp.dot(k_aug, qst_ref[hh], preferred_element_type=F32)
            if isinstance(r, int) and r == 0:
                s = s + mask_ref[...]
            s_ref[slot, hh] = s
            smax_ref[slot, hh] = jnp.max(s, axis=0, keepdims=True)

    def softmax(r, slot):
        j = block_of(r)
        for hh in heads:
            c = slopes[hh] * ((j - i) * tq).astype(F32)
            m_prev = m_ref[hh]
            m_new = jnp.maximum(m_prev, smax_ref[slot, hh] + c)
            alpha_ref[slot, hh] = jnp.exp(m_prev - m_new)
            p_ref[slot, hh] = jnp.exp(s_ref[slot, hh] - (m_new - c)).astype(BF16)
            m_ref[hh] = m_new

    def pv(r, slot):
        j = block_of(r)
        for hh in heads:
            acc_ref[hh] = alpha_ref[slot, hh] * acc_ref[hh] + jnp.dot(
                vt_ref[hh, j], p_ref[slot, hh], preferred_element_type=F32)

    def tick(u, parity, *, with_pv=True, with_qk=True):
        if with_pv:
            pv(u - 2, parity)
        softmax(u - 1, 1 - parity)
        if with_qk:
            qk(u, parity)

    def two_ticks(t, carry):
        u = 2 * t
        qk(u, 0)
        softmax(u - 1, 1)
        pv(u - 2, 0)
        qk(u + 1, 1)
        softmax(u, 0)
        pv(u - 1, 1)
        return carry

    n = i + 1
    qk(0, 0)

    @pl.when(n >= 2)
    def _():
        tick(1, 1, with_pv=False)

    lax.fori_loop(1, jnp.maximum(n - 2, 0) // 2 + 1, two_ticks, 0)

    @pl.when((n >= 2) & (n % 2 == 0))
    def _():
        tick(n, 0, with_qk=False)
        pv(n - 1, 1)

    @pl.when((n >= 3) & (n % 2 == 1))
    def _():
        tick(n - 1, 0)
        tick(n, 1, with_qk=False)
        pv(n - 1, 0)

    @pl.when(n == 1)
    def _():
        softmax(0, 0)
        pv(0, 0)

    lam = _diff_lambda(lp_ref, lam_init)
    for hh in heads:
        o = acc_ref[hh, 0:DA_V_DIM, :] / acc_ref[hh, DA_V_DIM:DA_V_DIM + 1, :]
        d = o[:, 0:tq] - lam * o[:, tq:2 * tq]
        r = d * lax.rsqrt(jnp.mean(d * d, axis=0, keepdims=True) + LN_EPS) * g_ref[...] * (1.0 - lam_init)
        o_ref[:, hh * DA_V_DIM:(hh + 1) * DA_V_DIM] = r.T.astype(o_ref.dtype)


def _prompt_attention(q, k, vt, lp, subln_g_col, *, lam_init, tq):
    bsz, seq, _ = q.shape
    assert tq & (tq - 1) == 0 and tq <= 256 and seq % tq == 0 and DA_HEADS % HEADS_PER_STEP == 0
    nk = seq // tq
    vt_rows = DA_V_DIM + ONES_ROWS
    hw = HEADS_PER_STEP * DA_V_DIM
    return pl.pallas_call(
        functools.partial(_attn_kernel, tq=tq, lam_init=lam_init),
        grid=(bsz, DA_HEADS // HEADS_PER_STEP, nk),
        in_specs=[
            pl.BlockSpec(memory_space=pltpu.SMEM),
            pl.BlockSpec((None, tq, hw), lambda b, h, i: (b, i, h)),
            pl.BlockSpec((None, seq, hw), lambda b, h, i: (b, 0, h)),
            pl.BlockSpec((None, HEADS_PER_STEP, nk, vt_rows, tq), lambda b, h, i: (b, h, 0, 0, 0)),
            pl.BlockSpec((4, DA_HEAD_DIM), lambda b, h, i: (0, 0)),
            pl.BlockSpec((DA_V_DIM, 1), lambda b, h, i: (0, 0)),
        ],
        out_specs=pl.BlockSpec((None, tq, hw), lambda b, h, i: (b, i, h)),
        out_shape=jax.ShapeDtypeStruct((bsz, seq, DA_HEADS * DA_V_DIM), BF16),
        scratch_shapes=[
            pltpu.VMEM((HEADS_PER_STEP, 2 * DA_V_DIM, 2 * tq), BF16),
            pltpu.VMEM((tq, DA_V_DIM), BF16),
            pltpu.VMEM((tq, 2 * tq), F32),
            pltpu.VMEM((2, HEADS_PER_STEP, tq, 2 * tq), F32),
            pltpu.VMEM((2, HEADS_PER_STEP, 1, 2 * tq), F32),
            pltpu.VMEM((2, HEADS_PER_STEP, tq, 2 * tq), BF16),
            pltpu.VMEM((2, HEADS_PER_STEP, 1, 2 * tq), F32),
            pltpu.VMEM((HEADS_PER_STEP, 1, 2 * tq), F32),
            pltpu.VMEM((HEADS_PER_STEP, vt_rows, 2 * tq), F32),
        ],
        compiler_params=_params("parallel", "parallel", "arbitrary"),
        name="prompt_diff_attention",
    )(jnp.asarray(_alibi_slopes()), q, k, vt, lp, subln_g_col)


def _ret_kernel(qk_ref, v_ref, gr_ref, o_ref, sfin_ref, state_ref, decay_ref, cross_ref, kdec_ref, *, chunk):
    c = pl.program_id(1)
    log_g = _retention_log_decay()

    @pl.when(c == 0)
    def _():
        state_ref[...] = jnp.zeros_like(state_ref)
        row = lax.broadcasted_iota(jnp.int32, (chunk, chunk), 0)
        col = lax.broadcasted_iota(jnp.int32, (chunk, chunk), 1)
        diff = (row - col).astype(F32)
        idx_v = lax.broadcasted_iota(jnp.int32, (chunk, RET_V_DIM), 0).astype(F32)
        idx_k = lax.broadcasted_iota(jnp.int32, (chunk, RET_K_DIM), 0).astype(F32)
        for h in range(RET_HEADS):
            lg = float(log_g[h])
            decay_ref[h] = jnp.where(diff >= 0, jnp.exp(jnp.maximum(diff, 0.0) * lg), 0.0)
            cross_ref[h] = jnp.exp((idx_v + 1.0) * lg)
            kdec_ref[h] = jnp.exp((chunk - 1.0 - idx_k) * lg)

    for h in range(RET_HEADS):
        q = qk_ref[:, h * RET_K_DIM:(h + 1) * RET_K_DIM]
        k = qk_ref[:, (RET_HEADS + h) * RET_K_DIM:(RET_HEADS + h + 1) * RET_K_DIM]
        v = v_ref[:, h * RET_V_DIM:(h + 1) * RET_V_DIM]
        gr = gr_ref[:, h * RET_V_DIM:(h + 1) * RET_V_DIM]
        scores = lax.dot_general(q, k, _NT, preferred_element_type=F32) * decay_ref[h]
        state = state_ref[h]
        o = (jnp.dot(scores.astype(BF16), v, preferred_element_type=F32)
             + jnp.dot(q, state.astype(BF16), preferred_element_type=F32) * cross_ref[h])
        k_w = (k.astype(F32) * kdec_ref[h]).astype(BF16)
        new_state = math.exp(chunk * float(log_g[h])) * state + lax.dot_general(
            k_w, v, (((0,), (0,)), ((), ())), preferred_element_type=F32)
        state_ref[h] = new_state
        o_ref[:, h * RET_V_DIM:(h + 1) * RET_V_DIM] = (_rms_norm(o) * _silu(gr.astype(F32))).astype(o_ref.dtype)

    @pl.when(c == pl.num_programs(1) - 1)
    def _():
        sfin_ref[...] = state_ref[...]


def _prompt_retention(qk, v, gr, *, chunk):
    bsz, seq, _ = qk.shape
    row_spec = pl.BlockSpec((None, chunk, D_MODEL), lambda b, c: (b, c, 0))
    state_shape = (RET_HEADS, RET_K_DIM, RET_V_DIM)
    return pl.pallas_call(
        functools.partial(_ret_kernel, chunk=chunk),
        grid=(bsz, seq // chunk),
        in_specs=[row_spec, row_spec, row_spec],
        out_specs=[row_spec, pl.BlockSpec((None,) + state_shape, lambda b, c: (b, 0, 0, 0))],
        out_shape=[
            jax.ShapeDtypeStruct((bsz, seq, RET_HEADS * RET_V_DIM), BF16),
            jax.ShapeDtypeStruct((bsz,) + state_shape, F32),
        ],
        scratch_shapes=[
            pltpu.VMEM(state_shape, F32),
            pltpu.VMEM((RET_HEADS, chunk, chunk), F32),
            pltpu.VMEM((RET_HEADS, chunk, RET_V_DIM), F32),
            pltpu.VMEM((RET_HEADS, chunk, RET_K_DIM), F32),
        ],
        compiler_params=_params("parallel", "arbitrary"),
        name="prompt_retention",
    )(qk, v, gr)


def _merge_kernel(x_ref, oa_ref, ob_ref, ga_ref, gb_ref, wa_ref, wb_ref, wo_ref, g_ref, b_ref, o_ref, *, alpha):
    a = jnp.dot(oa_ref[...].astype(BF16), wa_ref[...], preferred_element_type=F32)
    b = jnp.dot(ob_ref[...].astype(BF16), wb_ref[...], preferred_element_type=F32)
    m = jax.nn.sigmoid(ga_ref[...].astype(F32)) * a + jax.nn.sigmoid(gb_ref[...].astype(F32)) * b
    y = alpha * x_ref[...] + jnp.dot(m.astype(BF16), wo_ref[...], preferred_element_type=F32)
    o_ref[...] = _layer_norm(y, g_ref[...], b_ref[...])


def _merge_sublayer(x, oa, ob, ga, gb, w_a, w_b, w_o, g, b, *, alpha, tm):
    n = x.shape[0]
    row_spec = pl.BlockSpec((tm, D_MODEL), lambda i: (i, 0))
    w_spec = pl.BlockSpec((D_MODEL, D_MODEL), lambda i: (0, 0))
    vec_spec = pl.BlockSpec((1, D_MODEL), lambda i: (0, 0))
    return pl.pallas_call(
        functools.partial(_merge_kernel, alpha=alpha),
        grid=(n // tm,),
        in_specs=[row_spec] * 5 + [w_spec] * 3 + [vec_spec] * 2,
        out_specs=row_spec,
        out_shape=jax.ShapeDtypeStruct((n, D_MODEL), F32),
        compiler_params=_params("parallel"),
        name="merge_sublayer",
    )(x, oa, ob, ga, gb, w_a, w_b, w_o, g, b)


def _dec_attn_kernel(pt_ref, q_ref, slope_ref, *refs, t_dec, lam_init, n_pp):
    del pt_ref
    k_refs, v_refs = refs[:n_pp], refs[n_pp:2 * n_pp]
    kn_ref, vn_ref, lp_ref, g_ref, o_ref, mb_ref, mbn_ref, m_ref, l_ref, acc_ref = refs[2 * n_pp:]
    p = pl.program_id(1)
    n_rows = 2 * t_dec * DA_HEADS
    page_rows = PAGE_SIZE * DA_HEADS

    @pl.when(p == 0)
    def _():
        m_ref[...] = jnp.full_like(m_ref, NEG_BIG)
        l_ref[...] = jnp.zeros_like(l_ref)
        acc_ref[...] = jnp.zeros_like(acc_ref)
        row = lax.broadcasted_iota(jnp.int32, (n_rows, page_rows), 0)
        col = lax.broadcasted_iota(jnp.int32, (n_rows, page_rows), 1)
        same_head = ((col - row) & (DA_HEADS - 1)) == 0
        tok = (row // DA_HEADS) % t_dec
        rel = (col // DA_HEADS - tok - PAST_LEN).astype(F32)
        mb_ref[...] = jnp.where(same_head, slope_ref[...] * rel, NEG_BIG)
        rown = lax.broadcasted_iota(jnp.int32, (n_rows, PAGE_SIZE), 0)
        coln = lax.broadcasted_iota(jnp.int32, (n_rows, PAGE_SIZE), 1)
        tokq = (rown // DA_HEADS) % t_dec
        tokk = coln // DA_HEADS
        ok = (((coln - rown) & (DA_HEADS - 1)) == 0) & (tokk <= tokq)
        mbn_ref[...] = jnp.where(ok, slope_ref[...] * (tokk - tokq).astype(F32), NEG_BIG)

    def update(blocks):
        q = q_ref[...]
        scores = [lax.dot_general(q, kf.astype(BF16), _NT, preferred_element_type=F32) + mb
                  for kf, _, mb in blocks]
        m_prev = m_ref[...]
        m_new = m_prev
        for s in scores:
            m_new = jnp.maximum(m_new, jnp.max(s, axis=-1, keepdims=True))
        alpha = jnp.exp(m_prev - m_new)
        l_new = alpha * l_ref[...]
        acc_new = alpha * acc_ref[...]
        for s, (_, vf, _) in zip(scores, blocks):
            pr = jnp.exp(s - m_new)
            l_new = l_new + jnp.sum(pr, axis=-1, keepdims=True)
            acc_new = acc_new + jnp.dot(pr.astype(BF16), vf.astype(BF16), preferred_element_type=F32)
        l_ref[...] = l_new
        acc_ref[...] = acc_new
        m_ref[...] = m_new

    slope = slope_ref[...]
    update([(k_refs[r][...].reshape(page_rows, DA_V_DIM), v_refs[r][...].reshape(page_rows, DA_V_DIM),
             mb_ref[...] + slope * ((p * n_pp + r) * PAGE_SIZE).astype(F32)) for r in range(n_pp)])

    @pl.when(p == pl.num_programs(1) - 1)
    def _():
        update([(kn_ref[...], vn_ref[...], mbn_ref[...])])
        o = acc_ref[...] / l_ref[...]
        lam = _diff_lambda(lp_ref, lam_init)
        half = n_rows // 2
        d = o[0:half] - lam * o[half:n_rows]
        o_ref[...] = _rms_norm(d) * g_ref[...] * (1.0 - lam_init)


def _sample_attention(page_table, q_rows, cache_k, cache_v, k_new, v_new, lp, subln_g, *,
                      t_dec, lam_init, n_pp):
    n_dec, n_pages = page_table.shape
    n_rows = 2 * t_dec * DA_HEADS
    assert t_dec * DA_HEADS <= PAGE_SIZE and n_pages % n_pp == 0
    slope_rows = jnp.asarray(np.tile(_alibi_slopes(), 2 * t_dec).reshape(n_rows, 1))
    page_specs = [pl.BlockSpec((None, PAGE_SIZE, DA_HEADS, DA_V_DIM),
                               functools.partial(lambda s, p, pt, r: (pt[s, p * n_pp + r], 0, 0, 0), r=r))
                  for r in range(n_pp)]
    new_spec = pl.BlockSpec((None, PAGE_SIZE, DA_V_DIM), lambda s, p, pt: (s, 0, 0))
    return pl.pallas_call(
        functools.partial(_dec_attn_kernel, t_dec=t_dec, lam_init=lam_init, n_pp=n_pp),
        grid_spec=pltpu.PrefetchScalarGridSpec(
            num_scalar_prefetch=1,
            grid=(n_dec, n_pages // n_pp),
            in_specs=[
                pl.BlockSpec((None, n_rows, DA_V_DIM), lambda s, p, pt: (s, 0, 0)),
                pl.BlockSpec((n_rows, 1), lambda s, p, pt: (0, 0)),
                *page_specs, *page_specs, new_spec, new_spec,
                pl.BlockSpec((4, DA_HEAD_DIM), lambda s, p, pt: (0, 0)),
                pl.BlockSpec((1, DA_V_DIM), lambda s, p, pt: (0, 0)),
            ],
            out_specs=pl.BlockSpec((None, n_rows // 2, DA_V_DIM), lambda s, p, pt: (s, 0, 0)),
            scratch_shapes=[
                pltpu.VMEM((n_rows, PAGE_SIZE * DA_HEADS), F32),
                pltpu.VMEM((n_rows, PAGE_SIZE), F32),
                pltpu.VMEM((n_rows, 1), F32),
                pltpu.VMEM((n_rows, 1), F32),
                pltpu.VMEM((n_rows, DA_V_DIM), F32),
            ],
        ),
        out_shape=jax.ShapeDtypeStruct((n_dec, n_rows // 2, DA_V_DIM), F32),
        compiler_params=_params("parallel", "arbitrary"),
        name="sample_diff_attention",
    )(page_table, q_rows, slope_rows, *([cache_k] * n_pp), *([cache_v] * n_pp), k_new, v_new, lp, subln_g)


def _dec_ret_kernel(qk_ref, kt_ref, v_ref, gr_ref, st_ref, o_ref, snew_ref, *, t_dec, t_pad):
    log_g = _retention_log_decay()
    qk_all = qk_ref[...]
    v_all = v_ref[...]
    gr_all = gr_ref[...]
    tok = lax.broadcasted_iota(jnp.int32, (t_pad, 1), 0).astype(F32)
    for h in range(RET_HEADS):
        lg = float(log_g[h])
        q = qk_all[:, h * RET_K_DIM:(h + 1) * RET_K_DIM]
        k = qk_all[:, (RET_HEADS + h) * RET_K_DIM:(RET_HEADS + h + 1) * RET_K_DIM]
        v = v_all[:, h * RET_V_DIM:(h + 1) * RET_V_DIM]
        gr = gr_all[:, h * RET_V_DIM:(h + 1) * RET_V_DIM]
        kt = kt_ref[h]
        state = st_ref[h]
        o = jnp.dot(q.astype(BF16), state.astype(BF16), preferred_element_type=F32) * jnp.exp((tok + 1.0) * lg)
        new_state = math.exp(t_dec * lg) * state
        for j in range(t_dec):
            kj = kt[:, j:j + 1]
            vj = v[j:j + 1, :]
            s_j = jnp.sum(q * k[j:j + 1, :], axis=-1, keepdims=True)
            d_j = jnp.where(tok >= j, jnp.exp(jnp.maximum(tok - j, 0.0) * lg), 0.0)
            o = o + (s_j * d_j) * vj
            new_state = new_state + (kj * math.exp((t_dec - 1.0 - j) * lg)) * vj
        snew_ref[h] = new_state
        o_ref[:, h * RET_V_DIM:(h + 1) * RET_V_DIM] = _rms_norm(o) * _silu(gr)


def _sample_retention(qk, kt, v, gr, state, *, t_dec):
    n_dec, t_pad, _ = qk.shape
    return pl.pallas_call(
        functools.partial(_dec_ret_kernel, t_dec=t_dec, t_pad=t_pad),
        grid=(n_dec,),
        in_specs=[
            pl.BlockSpec((None, t_pad, 2 * RET_HEADS * RET_K_DIM), lambda s: (s, 0, 0)),
            pl.BlockSpec((None, RET_HEADS, RET_K_DIM, t_pad), lambda s: (s, 0, 0, 0)),
            pl.BlockSpec((None, t_pad, RET_HEADS * RET_V_DIM), lambda s: (s, 0, 0)),
            pl.BlockSpec((None, t_pad, RET_HEADS * RET_V_DIM), lambda s: (s, 0, 0)),
            pl.BlockSpec((None, RET_HEADS, RET_K_DIM, RET_V_DIM), lambda s: (s, 0, 0, 0)),
        ],
        out_specs=[
            pl.BlockSpec((None, t_pad, RET_HEADS * RET_V_DIM), lambda s: (s, 0, 0)),
            pl.BlockSpec((None, RET_HEADS, RET_K_DIM, RET_V_DIM), lambda s: (s, 0, 0, 0)),
        ],
        out_shape=[
            jax.ShapeDtypeStruct((n_dec, t_pad, RET_HEADS * RET_V_DIM), F32),
            jax.ShapeDtypeStruct(state.shape, F32),
        ],
        compiler_params=_params("parallel"),
        name="sample_retention",
    )(qk, kt, v, gr, state)


def _sample_attention_inputs(q, k, v, n_dec, t_dec):
    qh = q.reshape(n_dec, t_dec, DA_HEADS, DA_V_DIM) * (DA_HEAD_DIM ** -0.5)
    first = jnp.arange(DA_V_DIM) < DA_HEAD_DIM
    q_rows = jnp.stack([jnp.where(first, qh, 0.0), jnp.where(first, 0.0, qh)], axis=1)
    q_rows = q_rows.reshape(n_dec, 2 * t_dec * DA_HEADS, DA_V_DIM).astype(BF16)
    pad = ((0, 0), (0, PAGE_SIZE - t_dec * DA_HEADS), (0, 0))
    k_new = jnp.pad(k.reshape(n_dec, t_dec * DA_HEADS, DA_V_DIM), pad)
    v_new = jnp.pad(v.reshape(n_dec, t_dec * DA_HEADS, DA_V_DIM), pad)
    return q_rows, k_new, v_new


def kernel(x_prompt, x_sample, cache_k, cache_v, state_ret, page_table, ln_g, ln_b, ffn1_w_gu, ffn1_w_down,
           w_in, da_lambda, da_subln_g, w_branch_a, w_branch_b, w_out, ffn2_w_gu, ffn2_w_down):
    depth = w_in.shape[0]
    bsz, seq, _ = x_prompt.shape
    n_dec, t_dec, _ = x_sample.shape
    alpha = (2.0 * depth) ** 0.25
    t_pad = 8
    tm_p, tm_s = 512, n_dec * t_dec
    tq = 256

    xp = x_prompt.reshape(bsz * seq, D_MODEL)
    xs = x_sample.reshape(n_dec * t_dec, D_MODEL)
    qkr_scale = jnp.concatenate([jnp.ones((RET_HEADS * RET_K_DIM,), F32),
                                 jnp.full((RET_HEADS * RET_K_DIM,), RET_K_DIM ** -0.5, F32)]).reshape(1, D_MODEL)
    outs = [[] for _ in range(6)]
    for l in range(depth):
        lam_init = _lambda_init(l)
        w1_gu, w1_d = ffn1_w_gu[l].astype(BF16), ffn1_w_down[l].astype(BF16)
        w2_gu, w2_d = ffn2_w_gu[l].astype(BF16), ffn2_w_down[l].astype(BF16)
        w_in_l = w_in[l].astype(BF16)
        w_a, w_b, w_o = (w[l].astype(BF16) for w in (w_branch_a, w_branch_b, w_out))
        ln = [(ln_g[l, s].reshape(1, D_MODEL), ln_b[l, s].reshape(1, D_MODEL)) for s in range(3)]
        lp = da_lambda[l].astype(F32)
        subln = da_subln_g[l].reshape(1, DA_V_DIM)

        xp = _ffn_sublayer(xp, w1_gu, w1_d, *ln[0], alpha=alpha, tm=tm_p)
        proj = _input_projection(xp, w_in_l, qkr_scale, act_dtype=BF16, tm=tq, vt_batch=bsz)
        k32, v32, vt = proj[N_GROUPS], proj[N_GROUPS + 1], proj[N_GROUPS + 2]
        as_seq = lambda t: t.reshape(bsz, seq, D_MODEL)
        oa = _prompt_attention(as_seq(proj[G_QA]), as_seq(proj[G_KA]), vt, lp, subln.reshape(DA_V_DIM, 1),
                               lam_init=lam_init, tq=tq)
        ob, s_fin = _prompt_retention(as_seq(proj[G_QKR]), as_seq(proj[G_VR]), as_seq(proj[G_GR]), chunk=256)
        xp = _merge_sublayer(xp, oa.reshape(bsz * seq, D_MODEL), ob.reshape(bsz * seq, D_MODEL),
                             proj[G_GA], proj[G_GB], w_a, w_b, w_o, *ln[1], alpha=alpha, tm=tm_p)
        xp = _ffn_sublayer(xp, w2_gu, w2_d, *ln[2], alpha=alpha, tm=tm_p)
        outs[0].append(k32.reshape(bsz, seq, DA_HEADS, DA_V_DIM))
        outs[1].append(v32.reshape(bsz, seq, DA_HEADS, DA_V_DIM))
        outs[2].append(s_fin)

        xs = _ffn_sublayer(xs, w1_gu, w1_d, *ln[0], alpha=alpha, tm=tm_s)
        proj = _input_projection(xs, w_in_l, qkr_scale, act_dtype=F32, tm=tm_s)
        k32, v32 = proj[N_GROUPS], proj[N_GROUPS + 1]
        q_rows, k_new, v_new = _sample_attention_inputs(proj[G_QA], k32, v32, n_dec, t_dec)
        oa = _sample_attention(page_table, q_rows, cache_k[l], cache_v[l], k_new, v_new, lp, subln,
                               t_dec=t_dec, lam_init=lam_init, n_pp=16)
        tok_pad = lambda t: jnp.pad(t.reshape(n_dec, t_dec, -1), ((0, 0), (0, t_pad - t_dec), (0, 0)))
        qk_r = tok_pad(proj[G_QKR])
        kt_r = qk_r[:, :, RET_HEADS * RET_K_DIM:].reshape(n_dec, t_pad, RET_HEADS, RET_K_DIM)
        kt_r = kt_r.transpose(0, 2, 3, 1)
        ob, s_new = _sample_retention(qk_r, kt_r, tok_pad(proj[G_VR]), tok_pad(proj[G_GR]), state_ret[l],
                                      t_dec=t_dec)
        ob = ob[:, :t_dec].reshape(n_dec * t_dec, D_MODEL)
        xs = _merge_sublayer(xs, oa.reshape(n_dec * t_dec, D_MODEL), ob, proj[G_GA], proj[G_GB],
                             w_a, w_b, w_o, *ln[1], alpha=alpha, tm=tm_s)
        xs = _ffn_sublayer(xs, w2_gu, w2_d, *ln[2], alpha=alpha, tm=tm_s)
        outs[3].append(k32.reshape(n_dec, t_dec, DA_HEADS, DA_V_DIM))
        outs[4].append(v32.reshape(n_dec, t_dec, DA_HEADS, DA_V_DIM))
        outs[5].append(s_new)

    k_p, v_p, s_p, k_s, v_s, s_s = (jnp.stack(o) for o in outs)
    return (xp.reshape(bsz, seq, D_MODEL), xs.reshape(n_dec, t_dec, D_MODEL), k_p, v_p, s_p, k_s, v_s, s_s)
```

```python
import functools
import math

import numpy as np
import jax
import jax.numpy as jnp
from jax import lax
from jax.experimental import pallas as pl
from jax.experimental.pallas import tpu as pltpu

D_MODEL = 1024
PAST_LEN = 8192
PAGE_SIZE = 128
DA_HEADS = 8
DA_HEAD_DIM = 64
DA_V_DIM = 2 * DA_HEAD_DIM
RET_HEADS = 4
RET_K_DIM = 128
RET_V_DIM = 256
D_FF = 2816
LN_EPS = 1e-5
N_GROUPS = 8
G_QA, G_KA, G_VA, G_QKR, G_VR, G_GR, G_GA, G_GB = range(N_GROUPS)

VMEM_LIMIT_BYTES = 48 * 1024 * 1024
NEG_BIG = -1e30
ONES_ROWS = 16
MXU_DIM = 256
FF_CHUNKS = tuple((s, min(3 * MXU_DIM, D_FF - s)) for s in range(0, D_FF, 3 * MXU_DIM))
assert all(size % MXU_DIM == 0 for _, size in FF_CHUNKS)
PROJ_CHUNK = 2 * MXU_DIM

F32 = jnp.float32
BF16 = jnp.bfloat16
_NT = (((1,), (1,)), ((), ()))


def _alibi_slopes():
    return (2.0 ** (-8.0 * np.arange(1, DA_HEADS + 1) / DA_HEADS)).astype(np.float32)


def _retention_log_decay():
    return np.log(1.0 - 2.0 ** (-5.0 - np.arange(RET_HEADS))).astype(np.float32)


def _lambda_init(layer):
    return 0.8 - 0.6 * math.exp(-0.3 * layer)


def _params(*semantics):
    return pltpu.CompilerParams(dimension_semantics=semantics, vmem_limit_bytes=VMEM_LIMIT_BYTES)


def _layer_norm(y, g, b):
    mu = jnp.mean(y, axis=-1, keepdims=True)
    d = y - mu
    var = jnp.mean(d * d, axis=-1, keepdims=True)
    return d * lax.rsqrt(var + LN_EPS) * g + b


def _rms_norm(o):
    return o * lax.rsqrt(jnp.mean(o * o, axis=-1, keepdims=True) + LN_EPS)


def _silu(a):
    return a * jax.nn.sigmoid(a)


def _diff_lambda(lp_ref, lam_init):
    lp = lp_ref[...]
    a = jnp.sum(lp[0:1] * lp[1:2], axis=-1, keepdims=True)
    b = jnp.sum(lp[2:3] * lp[3:4], axis=-1, keepdims=True)
    return jnp.exp(a) - jnp.exp(b) + lam_init


def _ffn_kernel(x_ref, wgu_ref, wd_ref, g_ref, b_ref, o_ref, *, alpha):
    x = x_ref[...]
    xb = x.astype(BF16)
    acc = None
    for start, size in FF_CHUNKS:
        gate = jnp.dot(xb, wgu_ref[:, start:start + size], preferred_element_type=F32)
        up = jnp.dot(xb, wgu_ref[:, D_FF + start:D_FF + start + size], preferred_element_type=F32)
        h = (_silu(gate) * up).astype(BF16)
        part = jnp.dot(h, wd_ref[start:start + size, :], preferred_element_type=F32)
        acc = part if acc is None else acc + part
    o_ref[...] = _layer_norm(alpha * x + 0.5 * acc, g_ref[...], b_ref[...])


def _ffn_sublayer(x, w_gu, w_down, g, b, *, alpha, tm):
    n = x.shape[0]
    resident = functools.partial(pl.BlockSpec, index_map=lambda i: (0, 0), pipeline_mode=pl.Buffered(1))
    return pl.pallas_call(
        functools.partial(_ffn_kernel, alpha=alpha),
        grid=(n // tm,),
        in_specs=[
            pl.BlockSpec((tm, D_MODEL), lambda i: (i, 0)),
            resident((D_MODEL, 2 * D_FF)),
            resident((D_FF, D_MODEL)),
            resident((1, D_MODEL)),
            resident((1, D_MODEL)),
        ],
        out_specs=pl.BlockSpec((tm, D_MODEL), lambda i: (i, 0)),
        out_shape=jax.ShapeDtypeStruct((n, D_MODEL), F32),
        compiler_params=_params("parallel"),
        name="ffn_sublayer",
    )(x, w_gu, w_down, g, b)


def _proj_kernel(x_ref, w_ref, scale_ref, *refs, emit_vt):
    out_refs = refs[:N_GROUPS]
    k32_ref, v32_ref = refs[N_GROUPS:N_GROUPS + 2]
    xb = x_ref[...].astype(BF16)
    for grp in range(N_GROUPS):
        for start in range(0, D_MODEL, PROJ_CHUNK):
            cols = slice(start, start + PROJ_CHUNK)
            y = jnp.dot(xb, w_ref[:, grp * D_MODEL + start:grp * D_MODEL + start + PROJ_CHUNK],
                        preferred_element_type=F32)
            if grp == G_QKR:
                y = y * scale_ref[:, cols]
            out_refs[grp][:, cols] = y.astype(out_refs[grp].dtype)
            if grp == G_KA:
                k32_ref[:, cols] = y
            if grp == G_VA:
                v32_ref[:, cols] = y
                if emit_vt:
                    vt_ref = refs[N_GROUPS + 2]
                    for hh in range(PROJ_CHUNK // DA_V_DIM):
                        head = start // DA_V_DIM + hh
                        vt_ref[head, 0:DA_V_DIM, :] = y[:, hh * DA_V_DIM:(hh + 1) * DA_V_DIM].T.astype(BF16)
                        vt_ref[head, DA_V_DIM:DA_V_DIM + ONES_ROWS, :] = jnp.ones(
                            (ONES_ROWS, y.shape[0]), BF16)


def _input_projection(x, w_in, qkr_scale, *, act_dtype, tm, vt_batch=None):
    n = x.shape[0]
    resident = functools.partial(pl.BlockSpec, index_map=lambda i: (0, 0), pipeline_mode=pl.Buffered(1))
    row_spec = pl.BlockSpec((tm, D_MODEL), lambda i: (i, 0))
    out_specs = [row_spec] * (N_GROUPS + 2)
    out_shape = [jax.ShapeDtypeStruct((n, D_MODEL), act_dtype)] * N_GROUPS + [
        jax.ShapeDtypeStruct((n, D_MODEL), F32)] * 2
    if vt_batch is not None:
        nk = n // vt_batch // tm
        vt_rows = DA_V_DIM + ONES_ROWS
        out_specs.append(pl.BlockSpec((None, DA_HEADS, None, vt_rows, tm), lambda i: (i // nk, 0, i % nk, 0, 0)))
        out_shape.append(jax.ShapeDtypeStruct((vt_batch, DA_HEADS, nk, vt_rows, tm), BF16))
    return pl.pallas_call(
        functools.partial(_proj_kernel, emit_vt=vt_batch is not None),
        grid=(n // tm,),
        in_specs=[row_spec, resident((D_MODEL, N_GROUPS * D_MODEL)), resident((1, D_MODEL))],
        out_specs=out_specs,
        out_shape=out_shape,
        compiler_params=_params("parallel"),
        name="input_projection",
    )(x, w_in, qkr_scale)


HEADS_PER_STEP = 4
LOG2E = math.log2(math.e)


def _attn_kernel(slopes_ref, q_ref, k_ref, vt_ref, lp_ref, g_ref, o_ref,
                 qst_ref, kfeat_ref, mask_ref, s_ref, smax_ref, p_ref, alpha_ref, m_ref, acc_ref, *, tq, lam_init):
    hp = pl.program_id(1)
    i = pl.program_id(2)
    heads = range(HEADS_PER_STEP)
    slopes = [slopes_ref[hp * HEADS_PER_STEP + hh] for hh in heads]

    @pl.when(i == 0)
    def _():
        sub = lax.broadcasted_iota(jnp.int32, (DA_V_DIM, 2 * tq), 0)
        qrel = (lax.broadcasted_iota(jnp.int32, (DA_V_DIM, 2 * tq), 1) & (tq - 1)).astype(F32)
        def split(x):
            hi = x.astype(BF16).astype(F32)
            return hi, (x - hi).astype(BF16).astype(F32)

        for hh in heads:
            key_hi, key_lo = split(jnp.zeros_like(qrel) + slopes[hh] * LOG2E)
            qry_hi, qry_lo = split(-(slopes[hh] * LOG2E) * qrel)
            feat = jnp.where(sub == 0, key_hi, jnp.where(sub == 1, key_lo,
                             jnp.where(sub == 2, qry_hi, jnp.where(sub == 3, qry_lo, 0.0))))
            qst_ref[hh, DA_V_DIM:2 * DA_V_DIM, :] = feat.astype(BF16)
        lane = lax.broadcasted_iota(jnp.int32, (tq, DA_V_DIM), 1)
        krel = lax.broadcasted_iota(jnp.int32, (tq, DA_V_DIM), 0).astype(F32)
        kfeat_ref[...] = jnp.where(lane < 2, krel, jnp.where(lane < 4, 1.0, 0.0)).astype(BF16)
        key = lax.broadcasted_iota(jnp.int32, (tq, 2 * tq), 0)
        qry = lax.broadcasted_iota(jnp.int32, (tq, 2 * tq), 1) & (tq - 1)
        mask_ref[...] = jnp.where(key <= qry, 0.0, NEG_BIG)

    for hh in heads:
        q = q_ref[:, hh * DA_V_DIM:(hh + 1) * DA_V_DIM]
        qt = (q.astype(F32) * (DA_HEAD_DIM ** -0.5 * LOG2E)).T.astype(BF16)
        sub = lax.broadcasted_iota(jnp.int32, qt.shape, 0)
        zero = jnp.zeros_like(qt)
        qst_ref[hh, 0:DA_V_DIM, 0:tq] = jnp.where(sub < DA_HEAD_DIM, qt, zero)
        qst_ref[hh, 0:DA_V_DIM, tq:2 * tq] = jnp.where(sub >= DA_HEAD_DIM, qt, zero)

    m_ref[...] = jnp.full_like(m_ref, NEG_BIG)
    acc_ref[...] = jnp.zeros_like(acc_ref)

    def block_of(r):
        if isinstance(r, int):
            return i if r == 0 else r - 1
        return jnp.where(r == 0, i, r - 1)

    def qk(r, slot):
        j = block_of(r)
        for hh in heads:
            k = k_ref[pl.ds(j * tq, tq), hh * DA_V_DIM:(hh + 1) * DA_V_DIM]
            k_aug = jnp.concatenate([k, kfeat_ref[...]], axis=1)
            s = jnp.dot(k_aug, qst_ref[hh], preferred_element_type=F32)
            if isinstance(r, int) and r == 0:
                s = s + mask_ref[...]
            s_ref[slot, hh] = s
            smax_ref[slot, hh] = jnp.max(s, axis=0, keepdims=True)

    def softmax(r, slot):
        j = block_of(r)
        for hh in heads:
            c = (slopes[hh] * LOG2E) * ((j - i) * tq).astype(F32)
            m_prev = m_ref[hh]
            m_new = jnp.maximum(m_prev, smax_ref[slot, hh] + c)
            alpha_ref[slot, hh] = jnp.exp2(m_prev - m_new)
            p_ref[slot, hh] = jnp.exp2(s_ref[slot, hh] - (m_new - c)).astype(BF16)
            m_ref[hh] = m_new

    def pv(r, slot):
        j = block_of(r)
        for hh in heads:
            acc_ref[hh] = alpha_ref[slot, hh] * acc_ref[hh] + jnp.dot(
                vt_ref[hh, j], p_ref[slot, hh], preferred_element_type=F32)

    def tick(u, parity, *, with_pv=True, with_qk=True):
        if with_pv:
            pv(u - 2, parity)
        softmax(u - 1, 1 - parity)
        if with_qk:
            qk(u, parity)

    def two_ticks(t, carry):
        u = 2 * t
        qk(u, 0)
        softmax(u - 1, 1)
        pv(u - 2, 0)
        qk(u + 1, 1)
        softmax(u, 0)
        pv(u - 1, 1)
        return carry

    n = i + 1
    qk(0, 0)

    @pl.when(n >= 2)
    def _():
        tick(1, 1, with_pv=False)

    lax.fori_loop(1, jnp.maximum(n - 2, 0) // 2 + 1, two_ticks, 0)

    @pl.when((n >= 2) & (n % 2 == 0))
    def _():
        tick(n, 0, with_qk=False)
        pv(n - 1, 1)

    @pl.when((n >= 3) & (n % 2 == 1))
    def _():
        tick(n - 1, 0)
        tick(n, 1, with_qk=False)
        pv(n - 1, 0)

    @pl.when(n == 1)
    def _():
        softmax(0, 0)
        pv(0, 0)

    lam = _diff_lambda(lp_ref, lam_init)
    for hh in heads:
        o = acc_ref[hh, 0:DA_V_DIM, :] / acc_ref[hh, DA_V_DIM:DA_V_DIM + 1, :]
        d = o[:, 0:tq] - lam * o[:, tq:2 * tq]
        r = d * lax.rsqrt(jnp.mean(d * d, axis=0, keepdims=True) + LN_EPS) * g_ref[...] * (1.0 - lam_init)
        o_ref[:, hh * DA_V_DIM:(hh + 1) * DA_V_DIM] = r.T.astype(o_ref.dtype)


def _prompt_attention(q, k, vt, lp, subln_g_col, *, lam_init, tq):
    bsz, seq, _ = q.shape
    assert tq & (tq - 1) == 0 and tq <= 256 and seq % tq == 0 and DA_HEADS % HEADS_PER_STEP == 0
    nk = seq // tq
    vt_rows = DA_V_DIM + ONES_ROWS
    hw = HEADS_PER_STEP * DA_V_DIM
    return pl.pallas_call(
        functools.partial(_attn_kernel, tq=tq, lam_init=lam_init),
        grid=(bsz, DA_HEADS // HEADS_PER_STEP, nk),
        in_specs=[
            pl.BlockSpec(memory_space=pltpu.SMEM),
            pl.BlockSpec((None, tq, hw), lambda b, h, i: (b, i, h)),
            pl.BlockSpec((None, seq, hw), lambda b, h, i: (b, 0, h)),
            pl.BlockSpec((None, HEADS_PER_STEP, nk, vt_rows, tq), lambda b, h, i: (b, h, 0, 0, 0)),
            pl.BlockSpec((4, DA_HEAD_DIM), lambda b, h, i: (0, 0)),
            pl.BlockSpec((DA_V_DIM, 1), lambda b, h, i: (0, 0)),
        ],
        out_specs=pl.BlockSpec((None, tq, hw), lambda b, h, i: (b, i, h)),
        out_shape=jax.ShapeDtypeStruct((bsz, seq, DA_HEADS * DA_V_DIM), BF16),
        scratch_shapes=[
            pltpu.VMEM((HEADS_PER_STEP, 2 * DA_V_DIM, 2 * tq), BF16),
            pltpu.VMEM((tq, DA_V_DIM), BF16),
            pltpu.VMEM((tq, 2 * tq), F32),
            pltpu.VMEM((2, HEADS_PER_STEP, tq, 2 * tq), F32),
            pltpu.VMEM((2, HEADS_PER_STEP, 1, 2 * tq), F32),
            pltpu.VMEM((2, HEADS_PER_STEP, tq, 2 * tq), BF16),
            pltpu.VMEM((2, HEADS_PER_STEP, 1, 2 * tq), F32),
            pltpu.VMEM((HEADS_PER_STEP, 1, 2 * tq), F32),
            pltpu.VMEM((HEADS_PER_STEP, vt_rows, 2 * tq), F32),
        ],
        compiler_params=_params("parallel", "parallel", "arbitrary"),
        name="prompt_diff_attention",
    )(jnp.asarray(_alibi_slopes()), q, k, vt, lp, subln_g_col)


def _ret_kernel(qk_ref, v_ref, gr_ref, o_ref, sfin_ref, state_ref, decay_ref, cross_ref, kdec_ref, *, chunk):
    c = pl.program_id(1)
    log_g = _retention_log_decay()

    @pl.when(c == 0)
    def _():
        state_ref[...] = jnp.zeros_like(state_ref)
        row = lax.broadcasted_iota(jnp.int32, (chunk, chunk), 0)
        col = lax.broadcasted_iota(jnp.int32, (chunk, chunk), 1)
        diff = (row - col).astype(F32)
        idx_v = lax.broadcasted_iota(jnp.int32, (chunk, RET_V_DIM), 0).astype(F32)
        idx_k = lax.broadcasted_iota(jnp.int32, (chunk, RET_K_DIM), 0).astype(F32)
        for h in range(RET_HEADS):
            lg = float(log_g[h])
            decay_ref[h] = jnp.where(diff >= 0, jnp.exp(jnp.maximum(diff, 0.0) * lg), 0.0)
            cross_ref[h] = jnp.exp((idx_v + 1.0) * lg)
            kdec_ref[h] = jnp.exp((chunk - 1.0 - idx_k) * lg)

    for h in range(RET_HEADS):
        q = qk_ref[:, h * RET_K_DIM:(h + 1) * RET_K_DIM]
        k = qk_ref[:, (RET_HEADS + h) * RET_K_DIM:(RET_HEADS + h + 1) * RET_K_DIM]
        v = v_ref[:, h * RET_V_DIM:(h + 1) * RET_V_DIM]
        gr = gr_ref[:, h * RET_V_DIM:(h + 1) * RET_V_DIM]
        scores = lax.dot_general(q, k, _NT, preferred_element_type=F32) * decay_ref[h]
        state = state_ref[h]
        o = (jnp.dot(scores.astype(BF16), v, preferred_element_type=F32)
             + jnp.dot(q, state.astype(BF16), preferred_element_type=F32) * cross_ref[h])
        k_w = (k.astype(F32) * kdec_ref[h]).astype(BF16)
        new_state = math.exp(chunk * float(log_g[h])) * state + lax.dot_general(
            k_w, v, (((0,), (0,)), ((), ())), preferred_element_type=F32)
        state_ref[h] = new_state
        o_ref[:, h * RET_V_DIM:(h + 1) * RET_V_DIM] = (_rms_norm(o) * _silu(gr.astype(F32))).astype(o_ref.dtype)

    @pl.when(c == pl.num_programs(1) - 1)
    def _():
        sfin_ref[...] = state_ref[...]


def _prompt_retention(qk, v, gr, *, chunk):
    bsz, seq, _ = qk.shape
    row_spec = pl.BlockSpec((None, chunk, D_MODEL), lambda b, c: (b, c, 0))
    state_shape = (RET_HEADS, RET_K_DIM, RET_V_DIM)
    return pl.pallas_call(
        functools.partial(_ret_kernel, chunk=chunk),
        grid=(bsz, seq // chunk),
        in_specs=[row_spec, row_spec, row_spec],
        out_specs=[row_spec, pl.BlockSpec((None,) + state_shape, lambda b, c: (b, 0, 0, 0))],
        out_shape=[
            jax.ShapeDtypeStruct((bsz, seq, RET_HEADS * RET_V_DIM), BF16),
            jax.ShapeDtypeStruct((bsz,) + state_shape, F32),
        ],
        scratch_shapes=[
            pltpu.VMEM(state_shape, F32),
            pltpu.VMEM((RET_HEADS, chunk, chunk), F32),
            pltpu.VMEM((RET_HEADS, chunk, RET_V_DIM), F32),
            pltpu.VMEM((RET_HEADS, chunk, RET_K_DIM), F32),
        ],
        compiler_params=_params("parallel", "arbitrary"),
        name="prompt_retention",
    )(qk, v, gr)


def _merge_kernel(x_ref, oa_ref, ob_ref, ga_ref, gb_ref, wa_ref, wb_ref, wo_ref, g_ref, b_ref, o_ref, *, alpha):
    a = jnp.dot(oa_ref[...].astype(BF16), wa_ref[...], preferred_element_type=F32)
    b = jnp.dot(ob_ref[...].astype(BF16), wb_ref[...], preferred_element_type=F32)
    m = jax.nn.sigmoid(ga_ref[...].astype(F32)) * a + jax.nn.sigmoid(gb_ref[...].astype(F32)) * b
    y = alpha * x_ref[...] + jnp.dot(m.astype(BF16), wo_ref[...], preferred_element_type=F32)
    o_ref[...] = _layer_norm(y, g_ref[...], b_ref[...])


def _merge_sublayer(x, oa, ob, ga, gb, w_a, w_b, w_o, g, b, *, alpha, tm):
    n = x.shape[0]
    row_spec = pl.BlockSpec((tm, D_MODEL), lambda i: (i, 0))
    w_spec = pl.BlockSpec((D_MODEL, D_MODEL), lambda i: (0, 0))
    vec_spec = pl.BlockSpec((1, D_MODEL), lambda i: (0, 0))
    return pl.pallas_call(
        functools.partial(_merge_kernel, alpha=alpha),
        grid=(n // tm,),
        in_specs=[row_spec] * 5 + [w_spec] * 3 + [vec_spec] * 2,
        out_specs=row_spec,
        out_shape=jax.ShapeDtypeStruct((n, D_MODEL), F32),
        compiler_params=_params("parallel"),
        name="merge_sublayer",
    )(x, oa, ob, ga, gb, w_a, w_b, w_o, g, b)


def _dec_attn_kernel(pt_ref, q_ref, slope_ref, *refs, t_dec, lam_init, n_pp):
    del pt_ref
    k_refs, v_refs = refs[:n_pp], refs[n_pp:2 * n_pp]
    kn_ref, vn_ref, lp_ref, g_ref, o_ref, mb_ref, mbn_ref, m_ref, l_ref, acc_ref = refs[2 * n_pp:]
    p = pl.program_id(1)
    n_rows = 2 * t_dec * DA_HEADS
    page_rows = PAGE_SIZE * DA_HEADS

    @pl.when(p == 0)
    def _():
        m_ref[...] = jnp.full_like(m_ref, NEG_BIG)
        l_ref[...] = jnp.zeros_like(l_ref)
        acc_ref[...] = jnp.zeros_like(acc_ref)
        row = lax.broadcasted_iota(jnp.int32, (n_rows, page_rows), 0)
        col = lax.broadcasted_iota(jnp.int32, (n_rows, page_rows), 1)
        same_head = ((col - row) & (DA_HEADS - 1)) == 0
        tok = (row // DA_HEADS) % t_dec
        rel = (col // DA_HEADS - tok - PAST_LEN).astype(F32)
        mb_ref[...] = jnp.where(same_head, slope_ref[...] * rel, NEG_BIG)
        rown = lax.broadcasted_iota(jnp.int32, (n_rows, PAGE_SIZE), 0)
        coln = lax.broadcasted_iota(jnp.int32, (n_rows, PAGE_SIZE), 1)
        tokq = (rown // DA_HEADS) % t_dec
        tokk = coln // DA_HEADS
        ok = (((coln - rown) & (DA_HEADS - 1)) == 0) & (tokk <= tokq)
        mbn_ref[...] = jnp.where(ok, slope_ref[...] * (tokk - tokq).astype(F32), NEG_BIG)

    def update(blocks):
        q = q_ref[...]
        scores = [lax.dot_general(q, kf.astype(BF16), _NT, preferred_element_type=F32) + mb
                  for kf, _, mb in blocks]
        m_prev = m_ref[...]
        m_new = m_prev
        for s in scores:
            m_new = jnp.maximum(m_new, jnp.max(s, axis=-1, keepdims=True))
        alpha = jnp.exp(m_prev - m_new)
        l_new = alpha * l_ref[...]
        acc_new = alpha * acc_ref[...]
        for s, (_, vf, _) in zip(scores, blocks):
            pr = jnp.exp(s - m_new)
            l_new = l_new + jnp.sum(pr, axis=-1, keepdims=True)
            acc_new = acc_new + jnp.dot(pr.astype(BF16), vf.astype(BF16), preferred_element_type=F32)
        l_ref[...] = l_new
        acc_ref[...] = acc_new
        m_ref[...] = m_new

    slope = slope_ref[...]
    update([(k_refs[r][...].reshape(page_rows, DA_V_DIM), v_refs[r][...].reshape(page_rows, DA_V_DIM),
             mb_ref[...] + slope * ((p * n_pp + r) * PAGE_SIZE).astype(F32)) for r in range(n_pp)])

    @pl.when(p == pl.num_programs(1) - 1)
    def _():
        update([(kn_ref[...], vn_ref[...], mbn_ref[...])])
        o = acc_ref[...] / l_ref[...]
        lam = _diff_lambda(lp_ref, lam_init)
        half = n_rows // 2
        d = o[0:half] - lam * o[half:n_rows]
        o_ref[...] = _rms_norm(d) * g_ref[...] * (1.0 - lam_init)


def _sample_attention(page_table, q_rows, cache_k, cache_v, k_new, v_new, lp, subln_g, *,
                      t_dec, lam_init, n_pp):
    n_dec, n_pages = page_table.shape
    n_rows = 2 * t_dec * DA_HEADS
    assert t_dec * DA_HEADS <= PAGE_SIZE and n_pages % n_pp == 0
    slope_rows = jnp.asarray(np.tile(_alibi_slopes(), 2 * t_dec).reshape(n_rows, 1))
    page_specs = [pl.BlockSpec((None, PAGE_SIZE, DA_HEADS, DA_V_DIM),
                               functools.partial(lambda s, p, pt, r: (pt[s, p * n_pp + r], 0, 0, 0), r=r))
                  for r in range(n_pp)]
    new_spec = pl.BlockSpec((None, PAGE_SIZE, DA_V_DIM), lambda s, p, pt: (s, 0, 0))
    return pl.pallas_call(
        functools.partial(_dec_attn_kernel, t_dec=t_dec, lam_init=lam_init, n_pp=n_pp),
        grid_spec=pltpu.PrefetchScalarGridSpec(
            num_scalar_prefetch=1,
            grid=(n_dec, n_pages // n_pp),
            in_specs=[
                pl.BlockSpec((None, n_rows, DA_V_DIM), lambda s, p, pt: (s, 0, 0)),
                pl.BlockSpec((n_rows, 1), lambda s, p, pt: (0, 0)),
                *page_specs, *page_specs, new_spec, new_spec,
                pl.BlockSpec((4, DA_HEAD_DIM), lambda s, p, pt: (0, 0)),
                pl.BlockSpec((1, DA_V_DIM), lambda s, p, pt: (0, 0)),
            ],
            out_specs=pl.BlockSpec((None, n_rows // 2, DA_V_DIM), lambda s, p, pt: (s, 0, 0)),
            scratch_shapes=[
                pltpu.VMEM((n_rows, PAGE_SIZE * DA_HEADS), F32),
                pltpu.VMEM((n_rows, PAGE_SIZE), F32),
                pltpu.VMEM((n_rows, 1), F32),
                pltpu.VMEM((n_rows, 1), F32),
                pltpu.VMEM((n_rows, DA_V_DIM), F32),
            ],
        ),
        out_shape=jax.ShapeDtypeStruct((n_dec, n_rows // 2, DA_V_DIM), F32),
        compiler_params=_params("parallel", "arbitrary"),
        name="sample_diff_attention",
    )(page_table, q_rows, slope_rows, *([cache_k] * n_pp), *([cache_v] * n_pp), k_new, v_new, lp, subln_g)


def _dec_ret_kernel(qk_ref, kt_ref, v_ref, gr_ref, st_ref, o_ref, snew_ref, *, t_dec, t_pad):
    log_g = _retention_log_decay()
    qk_all = qk_ref[...]
    v_all = v_ref[...]
    gr_all = gr_ref[...]
    tok = lax.broadcasted_iota(jnp.int32, (t_pad, 1), 0).astype(F32)
    for h in range(RET_HEADS):
        lg = float(log_g[h])
        q = qk_all[:, h * RET_K_DIM:(h + 1) * RET_K_DIM]
        k = qk_all[:, (RET_HEADS + h) * RET_K_DIM:(RET_HEADS + h + 1) * RET_K_DIM]
        v = v_all[:, h * RET_V_DIM:(h + 1) * RET_V_DIM]
        gr = gr_all[:, h * RET_V_DIM:(h + 1) * RET_V_DIM]
        kt = kt_ref[h]
        state = st_ref[h]
        o = jnp.dot(q.astype(BF16), state.astype(BF16), preferred_element_type=F32) * jnp.exp((tok + 1.0) * lg)
        new_state = math.exp(t_dec * lg) * state
        for j in range(t_dec):
            kj = kt[:, j:j + 1]
            vj = v[j:j + 1, :]
            s_j = jnp.sum(q * k[j:j + 1, :], axis=-1, keepdims=True)
            d_j = jnp.where(tok >= j, jnp.exp(jnp.maximum(tok - j, 0.0) * lg), 0.0)
            o = o + (s_j * d_j) * vj
            new_state = new_state + (kj * math.exp((t_dec - 1.0 - j) * lg)) * vj
        snew_ref[h] = new_state
        o_ref[:, h * RET_V_DIM:(h + 1) * RET_V_DIM] = _rms_norm(o) * _silu(gr)


def _sample_retention(qk, kt, v, gr, state, *, t_dec):
    n_dec, t_pad, _ = qk.shape
    return pl.pallas_call(
        functools.partial(_dec_ret_kernel, t_dec=t_dec, t_pad=t_pad),
        grid=(n_dec,),
        in_specs=[
            pl.BlockSpec((None, t_pad, 2 * RET_HEADS * RET_K_DIM), lambda s: (s, 0, 0)),
            pl.BlockSpec((None, RET_HEADS, RET_K_DIM, t_pad), lambda s: (s, 0, 0, 0)),
            pl.BlockSpec((None, t_pad, RET_HEADS * RET_V_DIM), lambda s: (s, 0, 0)),
            pl.BlockSpec((None, t_pad, RET_HEADS * RET_V_DIM), lambda s: (s, 0, 0)),
            pl.BlockSpec((None, RET_HEADS, RET_K_DIM, RET_V_DIM), lambda s: (s, 0, 0, 0)),
        ],
        out_specs=[
            pl.BlockSpec((None, t_pad, RET_HEADS * RET_V_DIM), lambda s: (s, 0, 0)),
            pl.BlockSpec((None, RET_HEADS, RET_K_DIM, RET_V_DIM), lambda s: (s, 0, 0, 0)),
        ],
        out_shape=[
            jax.ShapeDtypeStruct((n_dec, t_pad, RET_HEADS * RET_V_DIM), F32),
            jax.ShapeDtypeStruct(state.shape, F32),
        ],
        compiler_params=_params("parallel"),
        name="sample_retention",
    )(qk, kt, v, gr, state)


def _sample_attention_inputs(q, k, v, n_dec, t_dec):
    qh = q.reshape(n_dec, t_dec, DA_HEADS, DA_V_DIM) * (DA_HEAD_DIM ** -0.5)
    first = jnp.arange(DA_V_DIM) < DA_HEAD_DIM
    q_rows = jnp.stack([jnp.where(first, qh, 0.0), jnp.where(first, 0.0, qh)], axis=1)
    q_rows = q_rows.reshape(n_dec, 2 * t_dec * DA_HEADS, DA_V_DIM).astype(BF16)
    pad = ((0, 0), (0, PAGE_SIZE - t_dec * DA_HEADS), (0, 0))
    k_new = jnp.pad(k.reshape(n_dec, t_dec * DA_HEADS, DA_V_DIM), pad)
    v_new = jnp.pad(v.reshape(n_dec, t_dec * DA_HEADS, DA_V_DIM), pad)
    return q_rows, k_new, v_new


def kernel(x_prompt, x_sample, cache_k, cache_v, state_ret, page_table, ln_g, ln_b, ffn1_w_gu, ffn1_w_down,
           w_in, da_lambda, da_subln_g, w_branch_a, w_branch_b, w_out, ffn2_w_gu, ffn2_w_down):
    depth = w_in.shape[0]
    bsz, seq, _ = x_prompt.shape
    n_dec, t_dec, _ = x_sample.shape
    alpha = (2.0 * depth) ** 0.25
    t_pad = 8
    tm_p, tm_s = 512, n_dec * t_dec
    tq = 256

    xp = x_prompt.reshape(bsz * seq, D_MODEL)
    xs = x_sample.reshape(n_dec * t_dec, D_MODEL)
    qkr_scale = jnp.concatenate([jnp.ones((RET_HEADS * RET_K_DIM,), F32),
                                 jnp.full((RET_HEADS * RET_K_DIM,), RET_K_DIM ** -0.5, F32)]).reshape(1, D_MODEL)
    outs = [[] for _ in range(6)]
    for l in range(depth):
        lam_init = _lambda_init(l)
        w1_gu, w1_d = ffn1_w_gu[l].astype(BF16), ffn1_w_down[l].astype(BF16)
        w2_gu, w2_d = ffn2_w_gu[l].astype(BF16), ffn2_w_down[l].astype(BF16)
        w_in_l = w_in[l].astype(BF16)
        w_a, w_b, w_o = (w[l].astype(BF16) for w in (w_branch_a, w_branch_b, w_out))
        ln = [(ln_g[l, s].reshape(1, D_MODEL), ln_b[l, s].reshape(1, D_MODEL)) for s in range(3)]
        lp = da_lambda[l].astype(F32)
        subln = da_subln_g[l].reshape(1, DA_V_DIM)

        xp = _ffn_sublayer(xp, w1_gu, w1_d, *ln[0], alpha=alpha, tm=tm_p)
        proj = _input_projection(xp, w_in_l, qkr_scale, act_dtype=BF16, tm=tq, vt_batch=bsz)
        k32, v32, vt = proj[N_GROUPS], proj[N_GROUPS + 1], proj[N_GROUPS + 2]
        as_seq = lambda t: t.reshape(bsz, seq, D_MODEL)
        oa = _prompt_attention(as_seq(proj[G_QA]), as_seq(proj[G_KA]), vt, lp, subln.reshape(DA_V_DIM, 1),
                               lam_init=lam_init, tq=tq)
        ob, s_fin = _prompt_retention(as_seq(proj[G_QKR]), as_seq(proj[G_VR]), as_seq(proj[G_GR]), chunk=256)
        xp = _merge_sublayer(xp, oa.reshape(bsz * seq, D_MODEL), ob.reshape(bsz * seq, D_MODEL),
                             proj[G_GA], proj[G_GB], w_a, w_b, w_o, *ln[1], alpha=alpha, tm=tm_p)
        xp = _ffn_sublayer(xp, w2_gu, w2_d, *ln[2], alpha=alpha, tm=tm_p)
        outs[0].append(k32.reshape(bsz, seq, DA_HEADS, DA_V_DIM))
        outs[1].append(v32.reshape(bsz, seq, DA_HEADS, DA_V_DIM))
        outs[2].append(s_fin)

        xs = _ffn_sublayer(xs, w1_gu, w1_d, *ln[0], alpha=alpha, tm=tm_s)
        proj = _input_projection(xs, w_in_l, qkr_scale, act_dtype=F32, tm=tm_s)
        k32, v32 = proj[N_GROUPS], proj[N_GROUPS + 1]
        q_rows, k_new, v_new = _sample_attention_inputs(proj[G_QA], k32, v32, n_dec, t_dec)
        oa = _sample_attention(page_table, q_rows, cache_k[l], cache_v[l], k_new, v_new, lp, subln,
                               t_dec=t_dec, lam_init=lam_init, n_pp=16)
        tok_pad = lambda t: jnp.pad(t.reshape(n_dec, t_dec, -1), ((0, 0), (0, t_pad - t_dec), (0, 0)))
        qk_r = tok_pad(proj[G_QKR])
        kt_r = qk_r[:, :, RET_HEADS * RET_K_DIM:].reshape(n_dec, t_pad, RET_HEADS, RET_K_DIM)
        kt_r = kt_r.transpose(0, 2, 3, 1)
        ob, s_new = _sample_retention(qk_r, kt_r, tok_pad(proj[G_VR]), tok_pad(proj[G_GR]), state_ret[l],
                                      t_dec=t_dec)
        ob = ob[:, :t_dec].reshape(n_dec * t_dec, D_MODEL)
        xs = _merge_sublayer(xs, oa.reshape(n_dec * t_dec, D_MODEL), ob, proj[G_GA], proj[G_GB],
                             w_a, w_b, w_o, *ln[1], alpha=alpha, tm=tm_s)
        xs = _ffn_sublayer(xs, w2_gu, w2_d, *ln[2], alpha=alpha, tm=tm_s)
        outs[3].append(k32.reshape(n_dec, t_dec, DA_HEADS, DA_V_DIM))
        outs[4].append(v32.reshape(n_dec, t_dec, DA_HEADS, DA_V_DIM))
        outs[5].append(s_new)

    k_p, v_p, s_p, k_s, v_s, s_s = (jnp.stack(o) for o in outs)
    return (xp.reshape(bsz, seq, D_MODEL), xs.reshape(n_dec, t_dec, D_MODEL), k_p, v_p, s_p, k_s, v_s, s_s)
```

```python
import functools
import math

import numpy as np
import jax
import jax.numpy as jnp
from jax import lax
from jax.experimental import pallas as pl
from jax.experimental.pallas import tpu as pltpu

D_MODEL = 1024
PAST_LEN = 8192
PAGE_SIZE = 128
DA_HEADS = 8
DA_HEAD_DIM = 64
DA_V_DIM = 2 * DA_HEAD_DIM
RET_HEADS = 4
RET_K_DIM = 128
RET_V_DIM = 256
D_FF = 2816
LN_EPS = 1e-5
N_GROUPS = 8
G_QA, G_KA, G_VA, G_QKR, G_VR, G_GR, G_GA, G_GB = range(N_GROUPS)

VMEM_LIMIT_BYTES = 48 * 1024 * 1024
NEG_BIG = -1e30
ONES_ROWS = 16
MXU_DIM = 256
FF_CHUNKS = tuple((s, min(3 * MXU_DIM, D_FF - s)) for s in range(0, D_FF, 3 * MXU_DIM))
assert all(size % MXU_DIM == 0 for _, size in FF_CHUNKS)
PROJ_CHUNK = 2 * MXU_DIM
SAMPLE_RET_SEQS = 4

F32 = jnp.float32
BF16 = jnp.bfloat16
_NT = (((1,), (1,)), ((), ()))


def _alibi_slopes():
    return (2.0 ** (-8.0 * np.arange(1, DA_HEADS + 1) / DA_HEADS)).astype(np.float32)


def _retention_log_decay():
    return np.log(1.0 - 2.0 ** (-5.0 - np.arange(RET_HEADS))).astype(np.float32)


def _lambda_init(layer):
    return 0.8 - 0.6 * math.exp(-0.3 * layer)


def _params(*semantics):
    return pltpu.CompilerParams(dimension_semantics=semantics, vmem_limit_bytes=VMEM_LIMIT_BYTES)


def _layer_norm(y, g, b):
    mu = jnp.mean(y, axis=-1, keepdims=True)
    d = y - mu
    var = jnp.mean(d * d, axis=-1, keepdims=True)
    return d * lax.rsqrt(var + LN_EPS) * g + b


def _rms_norm(o):
    return o * lax.rsqrt(jnp.mean(o * o, axis=-1, keepdims=True) + LN_EPS)


def _silu(a):
    return a * jax.nn.sigmoid(a)


def _diff_lambda(lp_ref, lam_init):
    lp = lp_ref[...]
    a = jnp.sum(lp[0:1] * lp[1:2], axis=-1, keepdims=True)
    b = jnp.sum(lp[2:3] * lp[3:4], axis=-1, keepdims=True)
    return jnp.exp(a) - jnp.exp(b) + lam_init


def _ffn_kernel(x_ref, wgu_ref, wd_ref, g_ref, b_ref, o_ref, *, alpha):
    x = x_ref[...]
    xb = x.astype(BF16)
    acc = None
    for start, size in FF_CHUNKS:
        gate = jnp.dot(xb, wgu_ref[:, start:start + size], preferred_element_type=F32)
        up = jnp.dot(xb, wgu_ref[:, D_FF + start:D_FF + start + size], preferred_element_type=F32)
        h = (_silu(gate) * up).astype(BF16)
        part = jnp.dot(h, wd_ref[start:start + size, :], preferred_element_type=F32)
        acc = part if acc is None else acc + part
    o_ref[...] = _layer_norm(alpha * x + 0.5 * acc, g_ref[...], b_ref[...])


def _ffn_sublayer(x, w_gu, w_down, g, b, *, alpha, tm):
    n = x.shape[0]
    resident = functools.partial(pl.BlockSpec, index_map=lambda i: (0, 0), pipeline_mode=pl.Buffered(1))
    return pl.pallas_call(
        functools.partial(_ffn_kernel, alpha=alpha),
        grid=(n // tm,),
        in_specs=[
            pl.BlockSpec((tm, D_MODEL), lambda i: (i, 0)),
            resident((D_MODEL, 2 * D_FF)),
            resident((D_FF, D_MODEL)),
            resident((1, D_MODEL)),
            resident((1, D_MODEL)),
        ],
        out_specs=pl.BlockSpec((tm, D_MODEL), lambda i: (i, 0)),
        out_shape=jax.ShapeDtypeStruct((n, D_MODEL), F32),
        compiler_params=_params("parallel"),
        name="ffn_sublayer",
    )(x, w_gu, w_down, g, b)


def _proj_kernel(x_ref, w_ref, scale_ref, *refs, emit_vt):
    out_refs = refs[:N_GROUPS]
    k32_ref, v32_ref = refs[N_GROUPS:N_GROUPS + 2]
    xb = x_ref[...].astype(BF16)
    for grp in range(N_GROUPS):
        for start in range(0, D_MODEL, PROJ_CHUNK):
            cols = slice(start, start + PROJ_CHUNK)
            y = jnp.dot(xb, w_ref[:, grp * D_MODEL + start:grp * D_MODEL + start + PROJ_CHUNK],
                        preferred_element_type=F32)
            if grp == G_QKR:
                y = y * scale_ref[:, cols]
            out_refs[grp][:, cols] = y.astype(out_refs[grp].dtype)
            if grp == G_KA:
                k32_ref[:, cols] = y
            if grp == G_VA:
                v32_ref[:, cols] = y
                if emit_vt:
                    vt_ref = refs[N_GROUPS + 2]
                    for hh in range(PROJ_CHUNK // DA_V_DIM):
                        head = start // DA_V_DIM + hh
                        vt_ref[head, 0:DA_V_DIM, :] = y[:, hh * DA_V_DIM:(hh + 1) * DA_V_DIM].T.astype(BF16)
                        vt_ref[head, DA_V_DIM:DA_V_DIM + ONES_ROWS, :] = jnp.ones(
                            (ONES_ROWS, y.shape[0]), BF16)


def _input_projection(x, w_in, qkr_scale, *, act_dtype, tm, vt_batch=None):
    n = x.shape[0]
    resident = functools.partial(pl.BlockSpec, index_map=lambda i: (0, 0), pipeline_mode=pl.Buffered(1))
    row_spec = pl.BlockSpec((tm, D_MODEL), lambda i: (i, 0))
    out_specs = [row_spec] * (N_GROUPS + 2)
    out_shape = [jax.ShapeDtypeStruct((n, D_MODEL), act_dtype)] * N_GROUPS + [
        jax.ShapeDtypeStruct((n, D_MODEL), F32)] * 2
    if vt_batch is not None:
        nk = n // vt_batch // tm
        vt_rows = DA_V_DIM + ONES_ROWS
        out_specs.append(pl.BlockSpec((None, DA_HEADS, None, vt_rows, tm), lambda i: (i // nk, 0, i % nk, 0, 0)))
        out_shape.append(jax.ShapeDtypeStruct((vt_batch, DA_HEADS, nk, vt_rows, tm), BF16))
    return pl.pallas_call(
        functools.partial(_proj_kernel, emit_vt=vt_batch is not None),
        grid=(n // tm,),
        in_specs=[row_spec, resident((D_MODEL, N_GROUPS * D_MODEL)), resident((1, D_MODEL))],
        out_specs=out_specs,
        out_shape=out_shape,
        compiler_params=_params("parallel"),
        name="input_projection",
    )(x, w_in, qkr_scale)


HEADS_PER_STEP = 4
LOG2E = math.log2(math.e)


def _attn_kernel(slopes_ref, q_ref, k_ref, vt_ref, lp_ref, g_ref, o_ref,
                 qst_ref, kfeat_ref, mask_ref, s_ref, smax_ref, p_ref, alpha_ref, m_ref, acc_ref, *, tq, lam_init):
    hp = pl.program_id(1)
    i = pl.program_id(2)
    heads = range(HEADS_PER_STEP)
    slopes = [slopes_ref[hp * HEADS_PER_STEP + hh] for hh in heads]

    @pl.when(i == 0)
    def _():
        sub = lax.broadcasted_iota(jnp.int32, (DA_V_DIM, 2 * tq), 0)
        qrel = (lax.broadcasted_iota(jnp.int32, (DA_V_DIM, 2 * tq), 1) & (tq - 1)).astype(F32)
        def split(x):
            hi = x.astype(BF16).astype(F32)
            return hi, (x - hi).astype(BF16).astype(F32)

        for hh in heads:
            key_hi, key_lo = split(jnp.zeros_like(qrel) + slopes[hh] * LOG2E)
            qry_hi, qry_lo = split(-(slopes[hh] * LOG2E) * qrel)
            feat = jnp.where(sub == 0, key_hi, jnp.where(sub == 1, key_lo,
                             jnp.where(sub == 2, qry_hi, jnp.where(sub == 3, qry_lo, 0.0))))
            qst_ref[hh, DA_V_DIM:2 * DA_V_DIM, :] = feat.astype(BF16)
        lane = lax.broadcasted_iota(jnp.int32, (tq, DA_V_DIM), 1)
        krel = lax.broadcasted_iota(jnp.int32, (tq, DA_V_DIM), 0).astype(F32)
        kfeat_ref[...] = jnp.where(lane < 2, krel, jnp.where(lane < 4, 1.0, 0.0)).astype(BF16)
        key = lax.broadcasted_iota(jnp.int32, (tq, 2 * tq), 0)
        qry = lax.broadcasted_iota(jnp.int32, (tq, 2 * tq), 1) & (tq - 1)
        mask_ref[...] = jnp.where(key <= qry, 0.0, NEG_BIG)

    for hh in heads:
        q = q_ref[:, hh * DA_V_DIM:(hh + 1) * DA_V_DIM]
        qt = (q.astype(F32) * (DA_HEAD_DIM ** -0.5 * LOG2E)).T.astype(BF16)
        sub = lax.broadcasted_iota(jnp.int32, qt.shape, 0)
        zero = jnp.zeros_like(qt)
        qst_ref[hh, 0:DA_V_DIM, 0:tq] = jnp.where(sub < DA_HEAD_DIM, qt, zero)
        qst_ref[hh, 0:DA_V_DIM, tq:2 * tq] = jnp.where(sub >= DA_HEAD_DIM, qt, zero)

    m_ref[...] = jnp.full_like(m_ref, NEG_BIG)
    acc_ref[...] = jnp.zeros_like(acc_ref)

    def block_of(r):
        if isinstance(r, int):
            return i if r == 0 else r - 1
        return jnp.where(r == 0, i, r - 1)

    def qk(r, slot):
        j = block_of(r)
        for hh in heads:
            k = k_ref[pl.ds(j * tq, tq), hh * DA_V_DIM:(hh + 1) * DA_V_DIM]
            k_aug = jnp.concatenate([k, kfeat_ref[...]], axis=1)
            s = jnp.dot(k_aug, qst_ref[hh], preferred_element_type=F32)
            if isinstance(r, int) and r == 0:
                s = s + mask_ref[...]
            s_ref[slot, hh] = s
            smax_ref[slot, hh] = jnp.max(s, axis=0, keepdims=True)

    def softmax(r, slot):
        j = block_of(r)
        for hh in heads:
            c = (slopes[hh] * LOG2E) * ((j - i) * tq).astype(F32)
            m_prev = m_ref[hh]
            m_new = jnp.maximum(m_prev, smax_ref[slot, hh] + c)
            alpha_ref[slot, hh] = jnp.exp2(m_prev - m_new)
            p_ref[slot, hh] = jnp.exp2(s_ref[slot, hh] - (m_new - c)).astype(BF16)
            m_ref[hh] = m_new

    def pv(r, slot):
        j = block_of(r)
        for hh in heads:
            acc_ref[hh] = alpha_ref[slot, hh] * acc_ref[hh] + jnp.dot(
                vt_ref[hh, j], p_ref[slot, hh], preferred_element_type=F32)

    def tick(u, parity, *, with_pv=True, with_qk=True):
        if with_pv:
            pv(u - 2, parity)
        softmax(u - 1, 1 - parity)
        if with_qk:
            qk(u, parity)

    def two_ticks(t, carry):
        u = 2 * t
        qk(u, 0)
        softmax(u - 1, 1)
        pv(u - 2, 0)
        qk(u + 1, 1)
        softmax(u, 0)
        pv(u - 1, 1)
        return carry

    n = i + 1
    qk(0, 0)

    @pl.when(n >= 2)
    def _():
        tick(1, 1, with_pv=False)

    lax.fori_loop(1, jnp.maximum(n - 2, 0) // 2 + 1, two_ticks, 0)

    @pl.when((n >= 2) & (n % 2 == 0))
    def _():
        tick(n, 0, with_qk=False)
        pv(n - 1, 1)

    @pl.when((n >= 3) & (n % 2 == 1))
    def _():
        tick(n - 1, 0)
        tick(n, 1, with_qk=False)
        pv(n - 1, 0)

    @pl.when(n == 1)
    def _():
        softmax(0, 0)
        pv(0, 0)

    lam = _diff_lambda(lp_ref, lam_init)
    for hh in heads:
        o = acc_ref[hh, 0:DA_V_DIM, :] / acc_ref[hh, DA_V_DIM:DA_V_DIM + 1, :]
        d = o[:, 0:tq] - lam * o[:, tq:2 * tq]
        r = d * lax.rsqrt(jnp.mean(d * d, axis=0, keepdims=True) + LN_EPS) * g_ref[...] * (1.0 - lam_init)
        o_ref[:, hh * DA_V_DIM:(hh + 1) * DA_V_DIM] = r.T.astype(o_ref.dtype)


def _prompt_attention(q, k, vt, lp, subln_g_col, *, lam_init, tq):
    bsz, seq, _ = q.shape
    assert tq & (tq - 1) == 0 and tq <= 256 and seq % tq == 0 and DA_HEADS % HEADS_PER_STEP == 0
    nk = seq // tq
    vt_rows = DA_V_DIM + ONES_ROWS
    hw = HEADS_PER_STEP * DA_V_DIM
    return pl.pallas_call(
        functools.partial(_attn_kernel, tq=tq, lam_init=lam_init),
        grid=(bsz, DA_HEADS // HEADS_PER_STEP, nk),
        in_specs=[
            pl.BlockSpec(memory_space=pltpu.SMEM),
            pl.BlockSpec((None, tq, hw), lambda b, h, i: (b, i, h)),
            pl.BlockSpec((None, seq, hw), lambda b, h, i: (b, 0, h)),
            pl.BlockSpec((None, HEADS_PER_STEP, nk, vt_rows, tq), lambda b, h, i: (b, h, 0, 0, 0)),
            pl.BlockSpec((4, DA_HEAD_DIM), lambda b, h, i: (0, 0)),
            pl.BlockSpec((DA_V_DIM, 1), lambda b, h, i: (0, 0)),
        ],
        out_specs=pl.BlockSpec((None, tq, hw), lambda b, h, i: (b, i, h)),
        out_shape=jax.ShapeDtypeStruct((bsz, seq, DA_HEADS * DA_V_DIM), BF16),
        scratch_shapes=[
            pltpu.VMEM((HEADS_PER_STEP, 2 * DA_V_DIM, 2 * tq), BF16),
            pltpu.VMEM((tq, DA_V_DIM), BF16),
            pltpu.VMEM((tq, 2 * tq), F32),
            pltpu.VMEM((2, HEADS_PER_STEP, tq, 2 * tq), F32),
            pltpu.VMEM((2, HEADS_PER_STEP, 1, 2 * tq), F32),
            pltpu.VMEM((2, HEADS_PER_STEP, tq, 2 * tq), BF16),
            pltpu.VMEM((2, HEADS_PER_STEP, 1, 2 * tq), F32),
            pltpu.VMEM((HEADS_PER_STEP, 1, 2 * tq), F32),
            pltpu.VMEM((HEADS_PER_STEP, vt_rows, 2 * tq), F32),
        ],
        compiler_params=_params("parallel", "parallel", "arbitrary"),
        name="prompt_diff_attention",
    )(jnp.asarray(_alibi_slopes()), q, k, vt, lp, subln_g_col)


def _ret_kernel(qk_ref, v_ref, gr_ref, o_ref, sfin_ref, state_ref, decay_ref, cross_ref, kdec_ref, *, chunk):
    c = pl.program_id(1)
    log_g = _retention_log_decay()

    @pl.when(c == 0)
    def _():
        state_ref[...] = jnp.zeros_like(state_ref)
        row = lax.broadcasted_iota(jnp.int32, (chunk, chunk), 0)
        col = lax.broadcasted_iota(jnp.int32, (chunk, chunk), 1)
        diff = (row - col).astype(F32)
        idx_v = lax.broadcasted_iota(jnp.int32, (chunk, RET_V_DIM), 0).astype(F32)
        idx_k = lax.broadcasted_iota(jnp.int32, (chunk, RET_K_DIM), 0).astype(F32)
        for h in range(RET_HEADS):
            lg = float(log_g[h])
            decay_ref[h] = jnp.where(diff >= 0, jnp.exp(jnp.maximum(diff, 0.0) * lg), 0.0)
            cross_ref[h] = jnp.exp((idx_v + 1.0) * lg)
            kdec_ref[h] = jnp.exp((chunk - 1.0 - idx_k) * lg)

    for h in range(RET_HEADS):
        q = qk_ref[:, h * RET_K_DIM:(h + 1) * RET_K_DIM]
        k = qk_ref[:, (RET_HEADS + h) * RET_K_DIM:(RET_HEADS + h + 1) * RET_K_DIM]
        v = v_ref[:, h * RET_V_DIM:(h + 1) * RET_V_DIM]
        gr = gr_ref[:, h * RET_V_DIM:(h + 1) * RET_V_DIM]
        scores = lax.dot_general(q, k, _NT, preferred_element_type=F32) * decay_ref[h]
        state = state_ref[h]
        o = (jnp.dot(scores.astype(BF16), v, preferred_element_type=F32)
             + jnp.dot(q, state.astype(BF16), preferred_element_type=F32) * cross_ref[h])
        k_w = (k.astype(F32) * kdec_ref[h]).astype(BF16)
        new_state = math.exp(chunk * float(log_g[h])) * state + lax.dot_general(
            k_w, v, (((0,), (0,)), ((), ())), preferred_element_type=F32)
        state_ref[h] = new_state
        o_ref[:, h * RET_V_DIM:(h + 1) * RET_V_DIM] = (_rms_norm(o) * _silu(gr.astype(F32))).astype(o_ref.dtype)

    @pl.when(c == pl.num_programs(1) - 1)
    def _():
        sfin_ref[...] = state_ref[...]


def _prompt_retention(qk, v, gr, *, chunk):
    bsz, seq, _ = qk.shape
    row_spec = pl.BlockSpec((None, chunk, D_MODEL), lambda b, c: (b, c, 0))
    state_shape = (RET_HEADS, RET_K_DIM, RET_V_DIM)
    return pl.pallas_call(
        functools.partial(_ret_kernel, chunk=chunk),
        grid=(bsz, seq // chunk),
        in_specs=[row_spec, row_spec, row_spec],
        out_specs=[row_spec, pl.BlockSpec((None,) + state_shape, lambda b, c: (b, 0, 0, 0))],
        out_shape=[
            jax.ShapeDtypeStruct((bsz, seq, RET_HEADS * RET_V_DIM), BF16),
            jax.ShapeDtypeStruct((bsz,) + state_shape, F32),
        ],
        scratch_shapes=[
            pltpu.VMEM(state_shape, F32),
            pltpu.VMEM((RET_HEADS, chunk, chunk), F32),
            pltpu.VMEM((RET_HEADS, chunk, RET_V_DIM), F32),
            pltpu.VMEM((RET_HEADS, chunk, RET_K_DIM), F32),
        ],
        compiler_params=_params("parallel", "arbitrary"),
        name="prompt_retention",
    )(qk, v, gr)


def _merge_kernel(x_ref, oa_ref, ob_ref, ga_ref, gb_ref, wa_ref, wb_ref, wo_ref, g_ref, b_ref, o_ref, *, alpha):
    a = jnp.dot(oa_ref[...].astype(BF16), wa_ref[...], preferred_element_type=F32)
    b = jnp.dot(ob_ref[...].astype(BF16), wb_ref[...], preferred_element_type=F32)
    m = jax.nn.sigmoid(ga_ref[...].astype(F32)) * a + jax.nn.sigmoid(gb_ref[...].astype(F32)) * b
    y = alpha * x_ref[...] + jnp.dot(m.astype(BF16), wo_ref[...], preferred_element_type=F32)
    o_ref[...] = _layer_norm(y, g_ref[...], b_ref[...])


def _merge_sublayer(x, oa, ob, ga, gb, w_a, w_b, w_o, g, b, *, alpha, tm):
    n = x.shape[0]
    row_spec = pl.BlockSpec((tm, D_MODEL), lambda i: (i, 0))
    w_spec = pl.BlockSpec((D_MODEL, D_MODEL), lambda i: (0, 0))
    vec_spec = pl.BlockSpec((1, D_MODEL), lambda i: (0, 0))
    return pl.pallas_call(
        functools.partial(_merge_kernel, alpha=alpha),
        grid=(n // tm,),
        in_specs=[row_spec] * 5 + [w_spec] * 3 + [vec_spec] * 2,
        out_specs=row_spec,
        out_shape=jax.ShapeDtypeStruct((n, D_MODEL), F32),
        compiler_params=_params("parallel"),
        name="merge_sublayer",
    )(x, oa, ob, ga, gb, w_a, w_b, w_o, g, b)


def _dec_attn_kernel(pt_ref, q_ref, slope_ref, *refs, t_dec, lam_init, n_pp):
    del pt_ref
    k_refs, v_refs = refs[:n_pp], refs[n_pp:2 * n_pp]
    kn_ref, vn_ref, lp_ref, g_ref, o_ref, mb_ref, mbn_ref, m_ref, l_ref, acc_ref = refs[2 * n_pp:]
    p = pl.program_id(1)
    n_rows = 2 * t_dec * DA_HEADS
    page_rows = PAGE_SIZE * DA_HEADS

    @pl.when(p == 0)
    def _():
        m_ref[...] = jnp.full_like(m_ref, NEG_BIG)
        l_ref[...] = jnp.zeros_like(l_ref)
        acc_ref[...] = jnp.zeros_like(acc_ref)
        row = lax.broadcasted_iota(jnp.int32, (n_rows, page_rows), 0)
        col = lax.broadcasted_iota(jnp.int32, (n_rows, page_rows), 1)
        same_head = ((col - row) & (DA_HEADS - 1)) == 0
        tok = (row // DA_HEADS) % t_dec
        rel = (col // DA_HEADS - tok - PAST_LEN).astype(F32)
        mb_ref[...] = jnp.where(same_head, slope_ref[...] * rel, NEG_BIG)
        rown = lax.broadcasted_iota(jnp.int32, (n_rows, PAGE_SIZE), 0)
        coln = lax.broadcasted_iota(jnp.int32, (n_rows, PAGE_SIZE), 1)
        tokq = (rown // DA_HEADS) % t_dec
        tokk = coln // DA_HEADS
        ok = (((coln - rown) & (DA_HEADS - 1)) == 0) & (tokk <= tokq)
        mbn_ref[...] = jnp.where(ok, slope_ref[...] * (tokk - tokq).astype(F32), NEG_BIG)

    def update(blocks):
        q = q_ref[...]
        scores = [lax.dot_general(q, kf.astype(BF16), _NT, preferred_element_type=F32) + mb
                  for kf, _, mb in blocks]
        m_prev = m_ref[...]
        m_new = m_prev
        for s in scores:
            m_new = jnp.maximum(m_new, jnp.max(s, axis=-1, keepdims=True))
        alpha = jnp.exp(m_prev - m_new)
        l_new = alpha * l_ref[...]
        acc_new = alpha * acc_ref[...]
        for s, (_, vf, _) in zip(scores, blocks):
            pr = jnp.exp(s - m_new)
            l_new = l_new + jnp.sum(pr, axis=-1, keepdims=True)
            acc_new = acc_new + jnp.dot(pr.astype(BF16), vf.astype(BF16), preferred_element_type=F32)
        l_ref[...] = l_new
        acc_ref[...] = acc_new
        m_ref[...] = m_new

    slope = slope_ref[...]
    update([(k_refs[r][...].reshape(page_rows, DA_V_DIM), v_refs[r][...].reshape(page_rows, DA_V_DIM),
             mb_ref[...] + slope * ((p * n_pp + r) * PAGE_SIZE).astype(F32)) for r in range(n_pp)])

    @pl.when(p == pl.num_programs(1) - 1)
    def _():
        update([(kn_ref[...], vn_ref[...], mbn_ref[...])])
        o = acc_ref[...] / l_ref[...]
        lam = _diff_lambda(lp_ref, lam_init)
        half = n_rows // 2
        d = o[0:half] - lam * o[half:n_rows]
        o_ref[...] = _rms_norm(d) * g_ref[...] * (1.0 - lam_init)


def _sample_attention(page_table, q_rows, cache_k, cache_v, k_new, v_new, lp, subln_g, *,
                      t_dec, lam_init, n_pp):
    n_dec, n_pages = page_table.shape
    n_rows = 2 * t_dec * DA_HEADS
    assert t_dec * DA_HEADS <= PAGE_SIZE and n_pages % n_pp == 0
    slope_rows = jnp.asarray(np.tile(_alibi_slopes(), 2 * t_dec).reshape(n_rows, 1))
    page_specs = [pl.BlockSpec((None, PAGE_SIZE, DA_HEADS, DA_V_DIM),
                               functools.partial(lambda s, p, pt, r: (pt[s, p * n_pp + r], 0, 0, 0), r=r))
                  for r in range(n_pp)]
    new_spec = pl.BlockSpec((None, PAGE_SIZE, DA_V_DIM), lambda s, p, pt: (s, 0, 0))
    return pl.pallas_call(
        functools.partial(_dec_attn_kernel, t_dec=t_dec, lam_init=lam_init, n_pp=n_pp),
        grid_spec=pltpu.PrefetchScalarGridSpec(
            num_scalar_prefetch=1,
            grid=(n_dec, n_pages // n_pp),
            in_specs=[
                pl.BlockSpec((None, n_rows, DA_V_DIM), lambda s, p, pt: (s, 0, 0)),
                pl.BlockSpec((n_rows, 1), lambda s, p, pt: (0, 0)),
                *page_specs, *page_specs, new_spec, new_spec,
                pl.BlockSpec((4, DA_HEAD_DIM), lambda s, p, pt: (0, 0)),
                pl.BlockSpec((1, DA_V_DIM), lambda s, p, pt: (0, 0)),
            ],
            out_specs=pl.BlockSpec((None, n_rows // 2, DA_V_DIM), lambda s, p, pt: (s, 0, 0)),
            scratch_shapes=[
                pltpu.VMEM((n_rows, PAGE_SIZE * DA_HEADS), F32),
                pltpu.VMEM((n_rows, PAGE_SIZE), F32),
                pltpu.VMEM((n_rows, 1), F32),
                pltpu.VMEM((n_rows, 1), F32),
                pltpu.VMEM((n_rows, DA_V_DIM), F32),
            ],
        ),
        out_shape=jax.ShapeDtypeStruct((n_dec, n_rows // 2, DA_V_DIM), F32),
        compiler_params=_params("parallel", "arbitrary"),
        name="sample_diff_attention",
    )(page_table, q_rows, slope_rows, *([cache_k] * n_pp), *([cache_v] * n_pp), k_new, v_new, lp, subln_g)


def _dec_ret_kernel(qk_ref, kt_ref, v_ref, gr_ref, st_ref, o_ref, snew_ref, *, t_dec, t_pad):
    for seq in range(qk_ref.shape[0]):
        _dec_ret_one(qk_ref.at[seq], kt_ref.at[seq], v_ref.at[seq], gr_ref.at[seq], st_ref.at[seq],
                     o_ref.at[seq], snew_ref.at[seq], t_dec=t_dec, t_pad=t_pad)


def _dec_ret_one(qk_ref, kt_ref, v_ref, gr_ref, st_ref, o_ref, snew_ref, *, t_dec, t_pad):
    log_g = _retention_log_decay()
    qk_all = qk_ref[...]
    v_all = v_ref[...]
    gr_all = gr_ref[...]
    tok = lax.broadcasted_iota(jnp.int32, (t_pad, 1), 0).astype(F32)
    for h in range(RET_HEADS):
        lg = float(log_g[h])
        q = qk_all[:, h * RET_K_DIM:(h + 1) * RET_K_DIM]
        k = qk_all[:, (RET_HEADS + h) * RET_K_DIM:(RET_HEADS + h + 1) * RET_K_DIM]
        v = v_all[:, h * RET_V_DIM:(h + 1) * RET_V_DIM]
        gr = gr_all[:, h * RET_V_DIM:(h + 1) * RET_V_DIM]
        kt = kt_ref[h]
        state = st_ref[h]
        o = jnp.dot(q.astype(BF16), state.astype(BF16), preferred_element_type=F32) * jnp.exp((tok + 1.0) * lg)
        new_state = math.exp(t_dec * lg) * state
        for j in range(t_dec):
            kj = kt[:, j:j + 1]
            vj = v[j:j + 1, :]
            s_j = jnp.sum(q * k[j:j + 1, :], axis=-1, keepdims=True)
            d_j = jnp.where(tok >= j, jnp.exp(jnp.maximum(tok - j, 0.0) * lg), 0.0)
            o = o + (s_j * d_j) * vj
            new_state = new_state + (kj * math.exp((t_dec - 1.0 - j) * lg)) * vj
        snew_ref[h] = new_state
        o_ref[:, h * RET_V_DIM:(h + 1) * RET_V_DIM] = _rms_norm(o) * _silu(gr)


def _sample_retention(qk, kt, v, gr, state, *, t_dec):
    n_dec, t_pad, _ = qk.shape
    seqs = math.gcd(n_dec, SAMPLE_RET_SEQS)
    rows = lambda width: pl.BlockSpec((seqs, t_pad, width), lambda s: (s, 0, 0))
    state_spec = pl.BlockSpec((seqs, RET_HEADS, RET_K_DIM, RET_V_DIM), lambda s: (s, 0, 0, 0))
    return pl.pallas_call(
        functools.partial(_dec_ret_kernel, t_dec=t_dec, t_pad=t_pad),
        grid=(n_dec // seqs,),
        in_specs=[
            rows(2 * RET_HEADS * RET_K_DIM),
            pl.BlockSpec((seqs, RET_HEADS, RET_K_DIM, t_pad), lambda s: (s, 0, 0, 0)),
            rows(RET_HEADS * RET_V_DIM),
            rows(RET_HEADS * RET_V_DIM),
            state_spec,
        ],
        out_specs=[rows(RET_HEADS * RET_V_DIM), state_spec],
        out_shape=[
            jax.ShapeDtypeStruct((n_dec, t_pad, RET_HEADS * RET_V_DIM), F32),
            jax.ShapeDtypeStruct(state.shape, F32),
        ],
        compiler_params=_params("parallel"),
        name="sample_retention",
    )(qk, kt, v, gr, state)


def _sample_attention_inputs(q, k, v, n_dec, t_dec):
    qh = q.reshape(n_dec, t_dec, DA_HEADS, DA_V_DIM) * (DA_HEAD_DIM ** -0.5)
    first = jnp.arange(DA_V_DIM) < DA_HEAD_DIM
    q_rows = jnp.stack([jnp.where(first, qh, 0.0), jnp.where(first, 0.0, qh)], axis=1)
    q_rows = q_rows.reshape(n_dec, 2 * t_dec * DA_HEADS, DA_V_DIM).astype(BF16)
    pad = ((0, 0), (0, PAGE_SIZE - t_dec * DA_HEADS), (0, 0))
    k_new = jnp.pad(k.reshape(n_dec, t_dec * DA_HEADS, DA_V_DIM), pad)
    v_new = jnp.pad(v.reshape(n_dec, t_dec * DA_HEADS, DA_V_DIM), pad)
    return q_rows, k_new, v_new


def kernel(x_prompt, x_sample, cache_k, cache_v, state_ret, page_table, ln_g, ln_b, ffn1_w_gu, ffn1_w_down,
           w_in, da_lambda, da_subln_g, w_branch_a, w_branch_b, w_out, ffn2_w_gu, ffn2_w_down):
    depth = w_in.shape[0]
    bsz, seq, _ = x_prompt.shape
    n_dec, t_dec, _ = x_sample.shape
    alpha = (2.0 * depth) ** 0.25
    t_pad = 8
    tm_p, tm_s = 512, n_dec * t_dec
    tq = 256

    xp = x_prompt.reshape(bsz * seq, D_MODEL)
    xs = x_sample.reshape(n_dec * t_dec, D_MODEL)
    qkr_scale = jnp.concatenate([jnp.ones((RET_HEADS * RET_K_DIM,), F32),
                                 jnp.full((RET_HEADS * RET_K_DIM,), RET_K_DIM ** -0.5, F32)]).reshape(1, D_MODEL)
    outs = [[] for _ in range(6)]
    for l in range(depth):
        lam_init = _lambda_init(l)
        w1_gu, w1_d = ffn1_w_gu[l].astype(BF16), ffn1_w_down[l].astype(BF16)
        w2_gu, w2_d = ffn2_w_gu[l].astype(BF16), ffn2_w_down[l].astype(BF16)
        w_in_l = w_in[l].astype(BF16)
        w_a, w_b, w_o = (w[l].astype(BF16) for w in (w_branch_a, w_branch_b, w_out))
        ln = [(ln_g[l, s].reshape(1, D_MODEL), ln_b[l, s].reshape(1, D_MODEL)) for s in range(3)]
        lp = da_lambda[l].astype(F32)
        subln = da_subln_g[l].reshape(1, DA_V_DIM)

        xp = _ffn_sublayer(xp, w1_gu, w1_d, *ln[0], alpha=alpha, tm=tm_p)
        proj = _input_projection(xp, w_in_l, qkr_scale, act_dtype=BF16, tm=tq, vt_batch=bsz)
        k32, v32, vt = proj[N_GROUPS], proj[N_GROUPS + 1], proj[N_GROUPS + 2]
        as_seq = lambda t: t.reshape(bsz, seq, D_MODEL)
        oa = _prompt_attention(as_seq(proj[G_QA]), as_seq(proj[G_KA]), vt, lp, subln.reshape(DA_V_DIM, 1),
                               lam_init=lam_init, tq=tq)
        ob, s_fin = _prompt_retention(as_seq(proj[G_QKR]), as_seq(proj[G_VR]), as_seq(proj[G_GR]), chunk=256)
        xp = _merge_sublayer(xp, oa.reshape(bsz * seq, D_MODEL), ob.reshape(bsz * seq, D_MODEL),
                             proj[G_GA], proj[G_GB], w_a, w_b, w_o, *ln[1], alpha=alpha, tm=tm_p)
        xp = _ffn_sublayer(xp, w2_gu, w2_d, *ln[2], alpha=alpha, tm=tm_p)
        outs[0].append(k32.reshape(bsz, seq, DA_HEADS, DA_V_DIM))
        outs[1].append(v32.reshape(bsz, seq, DA_HEADS, DA_V_DIM))
        outs[2].append(s_fin)

        xs = _ffn_sublayer(xs, w1_gu, w1_d, *ln[0], alpha=alpha, tm=tm_s)
        proj = _input_projection(xs, w_in_l, qkr_scale, act_dtype=F32, tm=tm_s)
        k32, v32 = proj[N_GROUPS], proj[N_GROUPS + 1]
        q_rows, k_new, v_new = _sample_attention_inputs(proj[G_QA], k32, v32, n_dec, t_dec)
        oa = _sample_attention(page_table, q_rows, cache_k[l], cache_v[l], k_new, v_new, lp, subln,
                               t_dec=t_dec, lam_init=lam_init, n_pp=16)
        tok_pad = lambda t: jnp.pad(t.reshape(n_dec, t_dec, -1), ((0, 0), (0, t_pad - t_dec), (0, 0)))
        qk_r = tok_pad(proj[G_QKR])
        kt_r = qk_r[:, :, RET_HEADS * RET_K_DIM:].reshape(n_dec, t_pad, RET_HEADS, RET_K_DIM)
        kt_r = kt_r.transpose(0, 2, 3, 1)
        ob, s_new = _sample_retention(qk_r, kt_r, tok_pad(proj[G_VR]), tok_pad(proj[G_GR]), state_ret[l],
                                      t_dec=t_dec)
        ob = ob[:, :t_dec].reshape(n_dec * t_dec, D_MODEL)
        xs = _merge_sublayer(xs, oa.reshape(n_dec * t_dec, D_MODEL), ob, proj[G_GA], proj[G_GB],
                             w_a, w_b, w_o, *ln[1], alpha=alpha, tm=tm_s)
        xs = _ffn_sublayer(xs, w2_gu, w2_d, *ln[2], alpha=alpha, tm=tm_s)
        outs[3].append(k32.reshape(n_dec, t_dec, DA_HEADS, DA_V_DIM))
        outs[4].append(v32.reshape(n_dec, t_dec, DA_HEADS, DA_V_DIM))
        outs[5].append(s_new)

    k_p, v_p, s_p, k_s, v_s, s_s = (jnp.stack(o) for o in outs)
    return (xp.reshape(bsz, seq, D_MODEL), xs.reshape(n_dec, t_dec, D_MODEL), k_p, v_p, s_p, k_s, v_s, s_s)
```

```python
import functools
import math

import numpy as np
import jax
import jax.numpy as jnp
from jax import lax
from jax.experimental import pallas as pl
from jax.experimental.pallas import tpu as pltpu

D_MODEL = 1024
PAST_LEN = 8192
PAGE_SIZE = 128
DA_HEADS = 8
DA_HEAD_DIM = 64
DA_V_DIM = 2 * DA_HEAD_DIM
RET_HEADS = 4
RET_K_DIM = 128
RET_V_DIM = 256
D_FF = 2816
LN_EPS = 1e-5
N_GROUPS = 8
G_QA, G_KA, G_VA, G_QKR, G_VR, G_GR, G_GA, G_GB = range(N_GROUPS)

VMEM_LIMIT_BYTES = 48 * 1024 * 1024
NEG_BIG = -1e30
ONES_ROWS = 16
MXU_DIM = 256
FF_CHUNKS = tuple((s, min(3 * MXU_DIM, D_FF - s)) for s in range(0, D_FF, 3 * MXU_DIM))
assert all(size % MXU_DIM == 0 for _, size in FF_CHUNKS)
PROJ_CHUNK = 2 * MXU_DIM
SAMPLE_RET_SEQS = 4

F32 = jnp.float32
BF16 = jnp.bfloat16
_NT = (((1,), (1,)), ((), ()))


def _alibi_slopes():
    return (2.0 ** (-8.0 * np.arange(1, DA_HEADS + 1) / DA_HEADS)).astype(np.float32)


def _retention_log_decay():
    return np.log(1.0 - 2.0 ** (-5.0 - np.arange(RET_HEADS))).astype(np.float32)


def _lambda_init(layer):
    return 0.8 - 0.6 * math.exp(-0.3 * layer)


def _params(*semantics):
    return pltpu.CompilerParams(dimension_semantics=semantics, vmem_limit_bytes=VMEM_LIMIT_BYTES)


def _layer_norm(y, g, b):
    mu = jnp.mean(y, axis=-1, keepdims=True)
    d = y - mu
    var = jnp.mean(d * d, axis=-1, keepdims=True)
    return d * lax.rsqrt(var + LN_EPS) * g + b


def _rms_norm(o):
    return o * lax.rsqrt(jnp.mean(o * o, axis=-1, keepdims=True) + LN_EPS)


def _silu(a):
    return a * jax.nn.sigmoid(a)


def _diff_lambda(lp_ref, lam_init):
    lp = lp_ref[...]
    a = jnp.sum(lp[0:1] * lp[1:2], axis=-1, keepdims=True)
    b = jnp.sum(lp[2:3] * lp[3:4], axis=-1, keepdims=True)
    return jnp.exp(a) - jnp.exp(b) + lam_init


def _ffn_rows(x_ref, wgu_ref, wd_ref, g_ref, b_ref, o_ref, alpha):
    x = x_ref[...]
    xb = x.astype(BF16)
    acc = None
    for start, size in FF_CHUNKS:
        gate = jnp.dot(xb, wgu_ref[:, start:start + size], preferred_element_type=F32)
        up = jnp.dot(xb, wgu_ref[:, D_FF + start:D_FF + start + size], preferred_element_type=F32)
        h = (_silu(gate) * up).astype(BF16)
        part = jnp.dot(h, wd_ref[start:start + size, :], preferred_element_type=F32)
        acc = part if acc is None else acc + part
    o_ref[...] = _layer_norm(alpha * x + 0.5 * acc, g_ref[...], b_ref[...])


def _ffn_kernel(xp_ref, xs_ref, wgu_ref, wd_ref, g_ref, b_ref, op_ref, os_ref, *, alpha):
    last = pl.num_programs(0) - 1

    @pl.when(pl.program_id(0) < last)
    def _():
        _ffn_rows(xp_ref, wgu_ref, wd_ref, g_ref, b_ref, op_ref, alpha)

    @pl.when(pl.program_id(0) == last)
    def _():
        _ffn_rows(xs_ref, wgu_ref, wd_ref, g_ref, b_ref, os_ref, alpha)


def _ffn_sublayer(xp, xs, w_gu, w_down, g, b, *, alpha, tm):
    n_tiles = xp.shape[0] // tm
    resident = functools.partial(pl.BlockSpec, index_map=lambda i: (0, 0), pipeline_mode=pl.Buffered(1))
    prompt_spec = pl.BlockSpec((tm, D_MODEL), lambda i: (jnp.minimum(i, n_tiles - 1), 0))
    sample_spec = pl.BlockSpec(xs.shape, lambda i: (0, 0))
    return pl.pallas_call(
        functools.partial(_ffn_kernel, alpha=alpha),
        grid=(n_tiles + 1,),
        in_specs=[
            prompt_spec,
            sample_spec,
            resident((D_MODEL, 2 * D_FF)),
            resident((D_FF, D_MODEL)),
            resident((1, D_MODEL)),
            resident((1, D_MODEL)),
        ],
        out_specs=[prompt_spec, sample_spec],
        out_shape=[jax.ShapeDtypeStruct(xp.shape, F32), jax.ShapeDtypeStruct(xs.shape, F32)],
        compiler_params=_params("arbitrary"),
        name="ffn_sublayer",
    )(xp, xs, w_gu, w_down, g, b)


def _proj_kernel(x_ref, w_ref, scale_ref, *refs, emit_vt):
    out_refs = refs[:N_GROUPS]
    k32_ref, v32_ref = refs[N_GROUPS:N_GROUPS + 2]
    xb = x_ref[...].astype(BF16)
    for grp in range(N_GROUPS):
        for start in range(0, D_MODEL, PROJ_CHUNK):
            cols = slice(start, start + PROJ_CHUNK)
            y = jnp.dot(xb, w_ref[:, grp * D_MODEL + start:grp * D_MODEL + start + PROJ_CHUNK],
                        preferred_element_type=F32)
            if grp == G_QKR:
                y = y * scale_ref[:, cols]
            out_refs[grp][:, cols] = y.astype(out_refs[grp].dtype)
            if grp == G_KA:
                k32_ref[:, cols] = y
            if grp == G_VA:
                v32_ref[:, cols] = y
                if emit_vt:
                    vt_ref = refs[N_GROUPS + 2]
                    for hh in range(PROJ_CHUNK // DA_V_DIM):
                        head = start // DA_V_DIM + hh
                        vt_ref[head, 0:DA_V_DIM, :] = y[:, hh * DA_V_DIM:(hh + 1) * DA_V_DIM].T.astype(BF16)
                        vt_ref[head, DA_V_DIM:DA_V_DIM + ONES_ROWS, :] = jnp.ones(
                            (ONES_ROWS, y.shape[0]), BF16)


def _input_projection(x, w_in, qkr_scale, *, act_dtype, tm, vt_batch=None):
    n = x.shape[0]
    resident = functools.partial(pl.BlockSpec, index_map=lambda i: (0, 0), pipeline_mode=pl.Buffered(1))
    row_spec = pl.BlockSpec((tm, D_MODEL), lambda i: (i, 0))
    out_specs = [row_spec] * (N_GROUPS + 2)
    out_shape = [jax.ShapeDtypeStruct((n, D_MODEL), act_dtype)] * N_GROUPS + [
        jax.ShapeDtypeStruct((n, D_MODEL), F32)] * 2
    if vt_batch is not None:
        nk = n // vt_batch // tm
        vt_rows = DA_V_DIM + ONES_ROWS
        out_specs.append(pl.BlockSpec((None, DA_HEADS, None, vt_rows, tm), lambda i: (i // nk, 0, i % nk, 0, 0)))
        out_shape.append(jax.ShapeDtypeStruct((vt_batch, DA_HEADS, nk, vt_rows, tm), BF16))
    return pl.pallas_call(
        functools.partial(_proj_kernel, emit_vt=vt_batch is not None),
        grid=(n // tm,),
        in_specs=[row_spec, resident((D_MODEL, N_GROUPS * D_MODEL)), resident((1, D_MODEL))],
        out_specs=out_specs,
        out_shape=out_shape,
        compiler_params=_params("parallel"),
        name="input_projection",
    )(x, w_in, qkr_scale)


HEADS_PER_STEP = 4
LOG2E = math.log2(math.e)


def _attn_kernel(slopes_ref, q_ref, k_ref, vt_ref, lp_ref, g_ref, o_ref,
                 qst_ref, kfeat_ref, mask_ref, s_ref, smax_ref, p_ref, alpha_ref, m_ref, acc_ref, *, tq, lam_init):
    hp = pl.program_id(1)
    i = pl.program_id(2)
    heads = range(HEADS_PER_STEP)
    slopes = [slopes_ref[hp * HEADS_PER_STEP + hh] for hh in heads]

    @pl.when(i == 0)
    def _():
        sub = lax.broadcasted_iota(jnp.int32, (DA_V_DIM, 2 * tq), 0)
        qrel = (lax.broadcasted_iota(jnp.int32, (DA_V_DIM, 2 * tq), 1) & (tq - 1)).astype(F32)
        def split(x):
            hi = x.astype(BF16).astype(F32)
            return hi, (x - hi).astype(BF16).astype(F32)

        for hh in heads:
            key_hi, key_lo = split(jnp.zeros_like(qrel) + slopes[hh] * LOG2E)
            qry_hi, qry_lo = split(-(slopes[hh] * LOG2E) * qrel)
            feat = jnp.where(sub == 0, key_hi, jnp.where(sub == 1, key_lo,
                             jnp.where(sub == 2, qry_hi, jnp.where(sub == 3, qry_lo, 0.0))))
            qst_ref[hh, DA_V_DIM:2 * DA_V_DIM, :] = feat.astype(BF16)
        lane = lax.broadcasted_iota(jnp.int32, (tq, DA_V_DIM), 1)
        krel = lax.broadcasted_iota(jnp.int32, (tq, DA_V_DIM), 0).astype(F32)
        kfeat_ref[...] = jnp.where(lane < 2, krel, jnp.where(lane < 4, 1.0, 0.0)).astype(BF16)
        key = lax.broadcasted_iota(jnp.int32, (tq, 2 * tq), 0)
        qry = lax.broadcasted_iota(jnp.int32, (tq, 2 * tq), 1) & (tq - 1)
        mask_ref[...] = jnp.where(key <= qry, 0.0, NEG_BIG)

    for hh in heads:
        q = q_ref[:, hh * DA_V_DIM:(hh + 1) * DA_V_DIM]
        qt = (q.astype(F32) * (DA_HEAD_DIM ** -0.5 * LOG2E)).T.astype(BF16)
        sub = lax.broadcasted_iota(jnp.int32, qt.shape, 0)
        zero = jnp.zeros_like(qt)
        qst_ref[hh, 0:DA_V_DIM, 0:tq] = jnp.where(sub < DA_HEAD_DIM, qt, zero)
        qst_ref[hh, 0:DA_V_DIM, tq:2 * tq] = jnp.where(sub >= DA_HEAD_DIM, qt, zero)

    m_ref[...] = jnp.full_like(m_ref, NEG_BIG)
    acc_ref[...] = jnp.zeros_like(acc_ref)

    def block_of(r):
        if isinstance(r, int):
            return i if r == 0 else r - 1
        return jnp.where(r == 0, i, r - 1)

    def qk(r, slot):
        j = block_of(r)
        for hh in heads:
            k = k_ref[pl.ds(j * tq, tq), hh * DA_V_DIM:(hh + 1) * DA_V_DIM]
            k_aug = jnp.concatenate([k, kfeat_ref[...]], axis=1)
            s = jnp.dot(k_aug, qst_ref[hh], preferred_element_type=F32)
            if isinstance(r, int) and r == 0:
                s = s + mask_ref[...]
            s_ref[slot, hh] = s
            smax_ref[slot, hh] = jnp.max(s, axis=0, keepdims=True)

    def softmax(r, slot):
        j = block_of(r)
        for hh in heads:
            c = (slopes[hh] * LOG2E) * ((j - i) * tq).astype(F32)
            m_prev = m_ref[hh]
            m_new = jnp.maximum(m_prev, smax_ref[slot, hh] + c)
            alpha_ref[slot, hh] = jnp.exp2(m_prev - m_new)
            p_ref[slot, hh] = jnp.exp2(s_ref[slot, hh] - (m_new - c)).astype(BF16)
            m_ref[hh] = m_new

    def pv(r, slot):
        j = block_of(r)
        for hh in heads:
            acc_ref[hh] = alpha_ref[slot, hh] * acc_ref[hh] + jnp.dot(
                vt_ref[hh, j], p_ref[slot, hh], preferred_element_type=F32)

    def tick(u, parity, *, with_pv=True, with_qk=True):
        if with_pv:
            pv(u - 2, parity)
        softmax(u - 1, 1 - parity)
        if with_qk:
            qk(u, parity)

    def two_ticks(t, carry):
        u = 2 * t
        qk(u, 0)
        softmax(u - 1, 1)
        pv(u - 2, 0)
        qk(u + 1, 1)
        softmax(u, 0)
        pv(u - 1, 1)
        return carry

    n = i + 1
    qk(0, 0)

    @pl.when(n >= 2)
    def _():
        tick(1, 1, with_pv=False)

    lax.fori_loop(1, jnp.maximum(n - 2, 0) // 2 + 1, two_ticks, 0)

    @pl.when((n >= 2) & (n % 2 == 0))
    def _():
        tick(n, 0, with_qk=False)
        pv(n - 1, 1)

    @pl.when((n >= 3) & (n % 2 == 1))
    def _():
        tick(n - 1, 0)
        tick(n, 1, with_qk=False)
        pv(n - 1, 0)

    @pl.when(n == 1)
    def _():
        softmax(0, 0)
        pv(0, 0)

    lam = _diff_lambda(lp_ref, lam_init)
    for hh in heads:
        o = acc_ref[hh, 0:DA_V_DIM, :] / acc_ref[hh, DA_V_DIM:DA_V_DIM + 1, :]
        d = o[:, 0:tq] - lam * o[:, tq:2 * tq]
        r = d * lax.rsqrt(jnp.mean(d * d, axis=0, keepdims=True) + LN_EPS) * g_ref[...] * (1.0 - lam_init)
        o_ref[:, hh * DA_V_DIM:(hh + 1) * DA_V_DIM] = r.T.astype(o_ref.dtype)


def _prompt_attention(q, k, vt, lp, subln_g_col, *, lam_init, tq):
    bsz, seq, _ = q.shape
    assert tq & (tq - 1) == 0 and tq <= 256 and seq % tq == 0 and DA_HEADS % HEADS_PER_STEP == 0
    nk = seq // tq
    vt_rows = DA_V_DIM + ONES_ROWS
    hw = HEADS_PER_STEP * DA_V_DIM
    return pl.pallas_call(
        functools.partial(_attn_kernel, tq=tq, lam_init=lam_init),
        grid=(bsz, DA_HEADS // HEADS_PER_STEP, nk),
        in_specs=[
            pl.BlockSpec(memory_space=pltpu.SMEM),
            pl.BlockSpec((None, tq, hw), lambda b, h, i: (b, i, h)),
            pl.BlockSpec((None, seq, hw), lambda b, h, i: (b, 0, h)),
            pl.BlockSpec((None, HEADS_PER_STEP, nk, vt_rows, tq), lambda b, h, i: (b, h, 0, 0, 0)),
            pl.BlockSpec((4, DA_HEAD_DIM), lambda b, h, i: (0, 0)),
            pl.BlockSpec((DA_V_DIM, 1), lambda b, h, i: (0, 0)),
        ],
        out_specs=pl.BlockSpec((None, tq, hw), lambda b, h, i: (b, i, h)),
        out_shape=jax.ShapeDtypeStruct((bsz, seq, DA_HEADS * DA_V_DIM), BF16),
        scratch_shapes=[
            pltpu.VMEM((HEADS_PER_STEP, 2 * DA_V_DIM, 2 * tq), BF16),
            pltpu.VMEM((tq, DA_V_DIM), BF16),
            pltpu.VMEM((tq, 2 * tq), F32),
            pltpu.VMEM((2, HEADS_PER_STEP, tq, 2 * tq), F32),
            pltpu.VMEM((2, HEADS_PER_STEP, 1, 2 * tq), F32),
            pltpu.VMEM((2, HEADS_PER_STEP, tq, 2 * tq), BF16),
            pltpu.VMEM((2, HEADS_PER_STEP, 1, 2 * tq), F32),
            pltpu.VMEM((HEADS_PER_STEP, 1, 2 * tq), F32),
            pltpu.VMEM((HEADS_PER_STEP, vt_rows, 2 * tq), F32),
        ],
        compiler_params=_params("parallel", "parallel", "arbitrary"),
        name="prompt_diff_attention",
    )(jnp.asarray(_alibi_slopes()), q, k, vt, lp, subln_g_col)


def _ret_kernel(qk_ref, v_ref, gr_ref, o_ref, sfin_ref, state_ref, decay_ref, cross_ref, kdec_ref, *, chunk):
    c = pl.program_id(1)
    log_g = _retention_log_decay()

    @pl.when(c == 0)
    def _():
        state_ref[...] = jnp.zeros_like(state_ref)
        row = lax.broadcasted_iota(jnp.int32, (chunk, chunk), 0)
        col = lax.broadcasted_iota(jnp.int32, (chunk, chunk), 1)
        diff = (row - col).astype(F32)
        idx_v = lax.broadcasted_iota(jnp.int32, (chunk, RET_V_DIM), 0).astype(F32)
        idx_k = lax.broadcasted_iota(jnp.int32, (chunk, RET_K_DIM), 0).astype(F32)
        for h in range(RET_HEADS):
            lg = float(log_g[h])
            decay_ref[h] = jnp.where(diff >= 0, jnp.exp(jnp.maximum(diff, 0.0) * lg), 0.0)
            cross_ref[h] = jnp.exp((idx_v + 1.0) * lg)
            kdec_ref[h] = jnp.exp((chunk - 1.0 - idx_k) * lg)

    for h in range(RET_HEADS):
        q = qk_ref[:, h * RET_K_DIM:(h + 1) * RET_K_DIM]
        k = qk_ref[:, (RET_HEADS + h) * RET_K_DIM:(RET_HEADS + h + 1) * RET_K_DIM]
        v = v_ref[:, h * RET_V_DIM:(h + 1) * RET_V_DIM]
        gr = gr_ref[:, h * RET_V_DIM:(h + 1) * RET_V_DIM]
        scores = lax.dot_general(q, k, _NT, preferred_element_type=F32) * decay_ref[h]
        state = state_ref[h]
        o = (jnp.dot(scores.astype(BF16), v, preferred_element_type=F32)
             + jnp.dot(q, state.astype(BF16), preferred_element_type=F32) * cross_ref[h])
        k_w = (k.astype(F32) * kdec_ref[h]).astype(BF16)
        new_state = math.exp(chunk * float(log_g[h])) * state + lax.dot_general(
            k_w, v, (((0,), (0,)), ((), ())), preferred_element_type=F32)
        state_ref[h] = new_state
        o_ref[:, h * RET_V_DIM:(h + 1) * RET_V_DIM] = (_rms_norm(o) * _silu(gr.astype(F32))).astype(o_ref.dtype)

    @pl.when(c == pl.num_programs(1) - 1)
    def _():
        sfin_ref[...] = state_ref[...]


def _prompt_retention(qk, v, gr, *, chunk):
    bsz, seq, _ = qk.shape
    row_spec = pl.BlockSpec((None, chunk, D_MODEL), lambda b, c: (b, c, 0))
    state_shape = (RET_HEADS, RET_K_DIM, RET_V_DIM)
    return pl.pallas_call(
        functools.partial(_ret_kernel, chunk=chunk),
        grid=(bsz, seq // chunk),
        in_specs=[row_spec, row_spec, row_spec],
        out_specs=[row_spec, pl.BlockSpec((None,) + state_shape, lambda b, c: (b, 0, 0, 0))],
        out_shape=[
            jax.ShapeDtypeStruct((bsz, seq, RET_HEADS * RET_V_DIM), BF16),
            jax.ShapeDtypeStruct((bsz,) + state_shape, F32),
        ],
        scratch_shapes=[
            pltpu.VMEM(state_shape, F32),
            pltpu.VMEM((RET_HEADS, chunk, chunk), F32),
            pltpu.VMEM((RET_HEADS, chunk, RET_V_DIM), F32),
            pltpu.VMEM((RET_HEADS, chunk, RET_K_DIM), F32),
        ],
        compiler_params=_params("parallel", "arbitrary"),
        name="prompt_retention",
    )(qk, v, gr)


def _merge_kernel(x_ref, oa_ref, ob_ref, ga_ref, gb_ref, wa_ref, wb_ref, wo_ref, g_ref, b_ref, o_ref, *, alpha):
    a = jnp.dot(oa_ref[...].astype(BF16), wa_ref[...], preferred_element_type=F32)
    b = jnp.dot(ob_ref[...].astype(BF16), wb_ref[...], preferred_element_type=F32)
    m = jax.nn.sigmoid(ga_ref[...].astype(F32)) * a + jax.nn.sigmoid(gb_ref[...].astype(F32)) * b
    y = alpha * x_ref[...] + jnp.dot(m.astype(BF16), wo_ref[...], preferred_element_type=F32)
    o_ref[...] = _layer_norm(y, g_ref[...], b_ref[...])


def _merge_sublayer(x, oa, ob, ga, gb, w_a, w_b, w_o, g, b, *, alpha, tm):
    n = x.shape[0]
    row_spec = pl.BlockSpec((tm, D_MODEL), lambda i: (i, 0))
    w_spec = pl.BlockSpec((D_MODEL, D_MODEL), lambda i: (0, 0))
    vec_spec = pl.BlockSpec((1, D_MODEL), lambda i: (0, 0))
    return pl.pallas_call(
        functools.partial(_merge_kernel, alpha=alpha),
        grid=(n // tm,),
        in_specs=[row_spec] * 5 + [w_spec] * 3 + [vec_spec] * 2,
        out_specs=row_spec,
        out_shape=jax.ShapeDtypeStruct((n, D_MODEL), F32),
        compiler_params=_params("parallel"),
        name="merge_sublayer",
    )(x, oa, ob, ga, gb, w_a, w_b, w_o, g, b)


def _dec_attn_kernel(pt_ref, q_ref, slope_ref, *refs, t_dec, lam_init, n_pp):
    del pt_ref
    k_refs, v_refs = refs[:n_pp], refs[n_pp:2 * n_pp]
    kn_ref, vn_ref, lp_ref, g_ref, o_ref, mb_ref, mbn_ref, m_ref, l_ref, acc_ref = refs[2 * n_pp:]
    p = pl.program_id(1)
    n_rows = 2 * t_dec * DA_HEADS
    page_rows = PAGE_SIZE * DA_HEADS

    @pl.when(p == 0)
    def _():
        m_ref[...] = jnp.full_like(m_ref, NEG_BIG)
        l_ref[...] = jnp.zeros_like(l_ref)
        acc_ref[...] = jnp.zeros_like(acc_ref)
        row = lax.broadcasted_iota(jnp.int32, (n_rows, page_rows), 0)
        col = lax.broadcasted_iota(jnp.int32, (n_rows, page_rows), 1)
        same_head = ((col - row) & (DA_HEADS - 1)) == 0
        tok = (row // DA_HEADS) % t_dec
        rel = (col // DA_HEADS - tok - PAST_LEN).astype(F32)
        mb_ref[...] = jnp.where(same_head, slope_ref[...] * rel, NEG_BIG)
        rown = lax.broadcasted_iota(jnp.int32, (n_rows, PAGE_SIZE), 0)
        coln = lax.broadcasted_iota(jnp.int32, (n_rows, PAGE_SIZE), 1)
        tokq = (rown // DA_HEADS) % t_dec
        tokk = coln // DA_HEADS
        ok = (((coln - rown) & (DA_HEADS - 1)) == 0) & (tokk <= tokq)
        mbn_ref[...] = jnp.where(ok, slope_ref[...] * (tokk - tokq).astype(F32), NEG_BIG)

    def update(blocks):
        q = q_ref[...]
        scores = [lax.dot_general(q, kf.astype(BF16), _NT, preferred_element_type=F32) + mb
                  for kf, _, mb in blocks]
        m_prev = m_ref[...]
        m_new = m_prev
        for s in scores:
            m_new = jnp.maximum(m_new, jnp.max(s, axis=-1, keepdims=True))
        alpha = jnp.exp(m_prev - m_new)
        l_new = alpha * l_ref[...]
        acc_new = alpha * acc_ref[...]
        for s, (_, vf, _) in zip(scores, blocks):
            pr = jnp.exp(s - m_new)
            l_new = l_new + jnp.sum(pr, axis=-1, keepdims=True)
            acc_new = acc_new + jnp.dot(pr.astype(BF16), vf.astype(BF16), preferred_element_type=F32)
        l_ref[...] = l_new
        acc_ref[...] = acc_new
        m_ref[...] = m_new

    slope = slope_ref[...]
    update([(k_refs[r][...].reshape(page_rows, DA_V_DIM), v_refs[r][...].reshape(page_rows, DA_V_DIM),
             mb_ref[...] + slope * ((p * n_pp + r) * PAGE_SIZE).astype(F32)) for r in range(n_pp)])

    @pl.when(p == pl.num_programs(1) - 1)
    def _():
        update([(kn_ref[...], vn_ref[...], mbn_ref[...])])
        o = acc_ref[...] / l_ref[...]
        lam = _diff_lambda(lp_ref, lam_init)
        half = n_rows // 2
        d = o[0:half] - lam * o[half:n_rows]
        o_ref[...] = _rms_norm(d) * g_ref[...] * (1.0 - lam_init)


def _sample_attention(page_table, q_rows, cache_k, cache_v, k_new, v_new, lp, subln_g, *,
                      t_dec, lam_init, n_pp):
    n_dec, n_pages = page_table.shape
    n_rows = 2 * t_dec * DA_HEADS
    assert t_dec * DA_HEADS <= PAGE_SIZE and n_pages % n_pp == 0
    slope_rows = jnp.asarray(np.tile(_alibi_slopes(), 2 * t_dec).reshape(n_rows, 1))
    page_specs = [pl.BlockSpec((None, PAGE_SIZE, DA_HEADS, DA_V_DIM),
                               functools.partial(lambda s, p, pt, r: (pt[s, p * n_pp + r], 0, 0, 0), r=r))
                  for r in range(n_pp)]
    new_spec = pl.BlockSpec((None, PAGE_SIZE, DA_V_DIM), lambda s, p, pt: (s, 0, 0))
    return pl.pallas_call(
        functools.partial(_dec_attn_kernel, t_dec=t_dec, lam_init=lam_init, n_pp=n_pp),
        grid_spec=pltpu.PrefetchScalarGridSpec(
            num_scalar_prefetch=1,
            grid=(n_dec, n_pages // n_pp),
            in_specs=[
                pl.BlockSpec((None, n_rows, DA_V_DIM), lambda s, p, pt: (s, 0, 0)),
                pl.BlockSpec((n_rows, 1), lambda s, p, pt: (0, 0)),
                *page_specs, *page_specs, new_spec, new_spec,
                pl.BlockSpec((4, DA_HEAD_DIM), lambda s, p, pt: (0, 0)),
                pl.BlockSpec((1, DA_V_DIM), lambda s, p, pt: (0, 0)),
            ],
            out_specs=pl.BlockSpec((None, n_rows // 2, DA_V_DIM), lambda s, p, pt: (s, 0, 0)),
            scratch_shapes=[
                pltpu.VMEM((n_rows, PAGE_SIZE * DA_HEADS), F32),
                pltpu.VMEM((n_rows, PAGE_SIZE), F32),
                pltpu.VMEM((n_rows, 1), F32),
                pltpu.VMEM((n_rows, 1), F32),
                pltpu.VMEM((n_rows, DA_V_DIM), F32),
            ],
        ),
        out_shape=jax.ShapeDtypeStruct((n_dec, n_rows // 2, DA_V_DIM), F32),
        compiler_params=_params("parallel", "arbitrary"),
        name="sample_diff_attention",
    )(page_table, q_rows, slope_rows, *([cache_k] * n_pp), *([cache_v] * n_pp), k_new, v_new, lp, subln_g)


def _dec_ret_kernel(qk_ref, kt_ref, v_ref, gr_ref, st_ref, o_ref, snew_ref, *, t_dec, t_pad):
    for seq in range(qk_ref.shape[0]):
        _dec_ret_one(qk_ref.at[seq], kt_ref.at[seq], v_ref.at[seq], gr_ref.at[seq], st_ref.at[seq],
                     o_ref.at[seq], snew_ref.at[seq], t_dec=t_dec, t_pad=t_pad)


def _dec_ret_one(qk_ref, kt_ref, v_ref, gr_ref, st_ref, o_ref, snew_ref, *, t_dec, t_pad):
    log_g = _retention_log_decay()
    qk_all = qk_ref[...]
    v_all = v_ref[...]
    gr_all = gr_ref[...]
    tok = lax.broadcasted_iota(jnp.int32, (t_pad, 1), 0).astype(F32)
    for h in range(RET_HEADS):
        lg = float(log_g[h])
        q = qk_all[:, h * RET_K_DIM:(h + 1) * RET_K_DIM]
        k = qk_all[:, (RET_HEADS + h) * RET_K_DIM:(RET_HEADS + h + 1) * RET_K_DIM]
        v = v_all[:, h * RET_V_DIM:(h + 1) * RET_V_DIM]
        gr = gr_all[:, h * RET_V_DIM:(h + 1) * RET_V_DIM]
        kt = kt_ref[h]
        state = st_ref[h]
        o = jnp.dot(q.astype(BF16), state.astype(BF16), preferred_element_type=F32) * jnp.exp((tok + 1.0) * lg)
        new_state = math.exp(t_dec * lg) * state
        for j in range(t_dec):
            kj = kt[:, j:j + 1]
            vj = v[j:j + 1, :]
            s_j = jnp.sum(q * k[j:j + 1, :], axis=-1, keepdims=True)
            d_j = jnp.where(tok >= j, jnp.exp(jnp.maximum(tok - j, 0.0) * lg), 0.0)
            o = o + (s_j * d_j) * vj
            new_state = new_state + (kj * math.exp((t_dec - 1.0 - j) * lg)) * vj
        snew_ref[h] = new_state
        o_ref[:, h * RET_V_DIM:(h + 1) * RET_V_DIM] = _rms_norm(o) * _silu(gr)


def _sample_retention(qk, kt, v, gr, state, *, t_dec):
    n_dec, t_pad, _ = qk.shape
    seqs = math.gcd(n_dec, SAMPLE_RET_SEQS)
    rows = lambda width: pl.BlockSpec((seqs, t_pad, width), lambda s: (s, 0, 0))
    state_spec = pl.BlockSpec((seqs, RET_HEADS, RET_K_DIM, RET_V_DIM), lambda s: (s, 0, 0, 0))
    return pl.pallas_call(
        functools.partial(_dec_ret_kernel, t_dec=t_dec, t_pad=t_pad),
        grid=(n_dec // seqs,),
        in_specs=[
            rows(2 * RET_HEADS * RET_K_DIM),
            pl.BlockSpec((seqs, RET_HEADS, RET_K_DIM, t_pad), lambda s: (s, 0, 0, 0)),
            rows(RET_HEADS * RET_V_DIM),
            rows(RET_HEADS * RET_V_DIM),
            state_spec,
        ],
        out_specs=[rows(RET_HEADS * RET_V_DIM), state_spec],
        out_shape=[
            jax.ShapeDtypeStruct((n_dec, t_pad, RET_HEADS * RET_V_DIM), F32),
            jax.ShapeDtypeStruct(state.shape, F32),
        ],
        compiler_params=_params("parallel"),
        name="sample_retention",
    )(qk, kt, v, gr, state)


def _sample_attention_inputs(q, k, v, n_dec, t_dec):
    qh = q.reshape(n_dec, t_dec, DA_HEADS, DA_V_DIM) * (DA_HEAD_DIM ** -0.5)
    first = jnp.arange(DA_V_DIM) < DA_HEAD_DIM
    q_rows = jnp.stack([jnp.where(first, qh, 0.0), jnp.where(first, 0.0, qh)], axis=1)
    q_rows = q_rows.reshape(n_dec, 2 * t_dec * DA_HEADS, DA_V_DIM).astype(BF16)
    pad = ((0, 0), (0, PAGE_SIZE - t_dec * DA_HEADS), (0, 0))
    k_new = jnp.pad(k.reshape(n_dec, t_dec * DA_HEADS, DA_V_DIM), pad)
    v_new = jnp.pad(v.reshape(n_dec, t_dec * DA_HEADS, DA_V_DIM), pad)
    return q_rows, k_new, v_new


def kernel(x_prompt, x_sample, cache_k, cache_v, state_ret, page_table, ln_g, ln_b, ffn1_w_gu, ffn1_w_down,
           w_in, da_lambda, da_subln_g, w_branch_a, w_branch_b, w_out, ffn2_w_gu, ffn2_w_down):
    depth = w_in.shape[0]
    bsz, seq, _ = x_prompt.shape
    n_dec, t_dec, _ = x_sample.shape
    alpha = (2.0 * depth) ** 0.25
    t_pad = 8
    tm_p, tm_s = 512, n_dec * t_dec
    tq = 256

    xp = x_prompt.reshape(bsz * seq, D_MODEL)
    xs = x_sample.reshape(n_dec * t_dec, D_MODEL)
    qkr_scale = jnp.concatenate([jnp.ones((RET_HEADS * RET_K_DIM,), F32),
                                 jnp.full((RET_HEADS * RET_K_DIM,), RET_K_DIM ** -0.5, F32)]).reshape(1, D_MODEL)
    outs = [[] for _ in range(6)]
    for l in range(depth):
        lam_init = _lambda_init(l)
        w1_gu, w1_d = ffn1_w_gu[l].astype(BF16), ffn1_w_down[l].astype(BF16)
        w2_gu, w2_d = ffn2_w_gu[l].astype(BF16), ffn2_w_down[l].astype(BF16)
        w_in_l = w_in[l].astype(BF16)
        w_a, w_b, w_o = (w[l].astype(BF16) for w in (w_branch_a, w_branch_b, w_out))
        ln = [(ln_g[l, s].reshape(1, D_MODEL), ln_b[l, s].reshape(1, D_MODEL)) for s in range(3)]
        lp = da_lambda[l].astype(F32)
        subln = da_subln_g[l].reshape(1, DA_V_DIM)

        xp, xs = _ffn_sublayer(xp, xs, w1_gu, w1_d, *ln[0], alpha=alpha, tm=tm_p)

        proj = _input_projection(xp, w_in_l, qkr_scale, act_dtype=BF16, tm=tq, vt_batch=bsz)
        k32, v32, vt = proj[N_GROUPS], proj[N_GROUPS + 1], proj[N_GROUPS + 2]
        as_seq = lambda t: t.reshape(bsz, seq, D_MODEL)
        oa = _prompt_attention(as_seq(proj[G_QA]), as_seq(proj[G_KA]), vt, lp, subln.reshape(DA_V_DIM, 1),
                               lam_init=lam_init, tq=tq)
        ob, s_fin = _prompt_retention(as_seq(proj[G_QKR]), as_seq(proj[G_VR]), as_seq(proj[G_GR]), chunk=256)
        xp = _merge_sublayer(xp, oa.reshape(bsz * seq, D_MODEL), ob.reshape(bsz * seq, D_MODEL),
                             proj[G_GA], proj[G_GB], w_a, w_b, w_o, *ln[1], alpha=alpha, tm=tm_p)
        outs[0].append(k32.reshape(bsz, seq, DA_HEADS, DA_V_DIM))
        outs[1].append(v32.reshape(bsz, seq, DA_HEADS, DA_V_DIM))
        outs[2].append(s_fin)

        proj = _input_projection(xs, w_in_l, qkr_scale, act_dtype=F32, tm=tm_s)
        k32, v32 = proj[N_GROUPS], proj[N_GROUPS + 1]
        q_rows, k_new, v_new = _sample_attention_inputs(proj[G_QA], k32, v32, n_dec, t_dec)
        oa = _sample_attention(page_table, q_rows, cache_k[l], cache_v[l], k_new, v_new, lp, subln,
                               t_dec=t_dec, lam_init=lam_init, n_pp=16)
        tok_pad = lambda t: jnp.pad(t.reshape(n_dec, t_dec, -1), ((0, 0), (0, t_pad - t_dec), (0, 0)))
        qk_r = tok_pad(proj[G_QKR])
        kt_r = qk_r[:, :, RET_HEADS * RET_K_DIM:].reshape(n_dec, t_pad, RET_HEADS, RET_K_DIM)
        kt_r = kt_r.transpose(0, 2, 3, 1)
        ob, s_new = _sample_retention(qk_r, kt_r, tok_pad(proj[G_VR]), tok_pad(proj[G_GR]), state_ret[l],
                                      t_dec=t_dec)
        ob = ob[:, :t_dec].reshape(n_dec * t_dec, D_MODEL)
        xs = _merge_sublayer(xs, oa.reshape(n_dec * t_dec, D_MODEL), ob, proj[G_GA], proj[G_GB],
                             w_a, w_b, w_o, *ln[1], alpha=alpha, tm=tm_s)
        xp, xs = _ffn_sublayer(xp, xs, w2_gu, w2_d, *ln[2], alpha=alpha, tm=tm_p)
        outs[3].append(k32.reshape(n_dec, t_dec, DA_HEADS, DA_V_DIM))
        outs[4].append(v32.reshape(n_dec, t_dec, DA_HEADS, DA_V_DIM))
        outs[5].append(s_new)

    k_p, v_p, s_p, k_s, v_s, s_s = (jnp.stack(o) for o in outs)
    return (xp.reshape(bsz, seq, D_MODEL), xs.reshape(n_dec, t_dec, D_MODEL), k_p, v_p, s_p, k_s, v_s, s_s)
```
